```python
import math
import jax
import jax.numpy as jnp
from jax import lax
import numpy as np

D_MODEL = 1024
BATCH = 8
SEQ = 2048
DEPTH = 4
DEC_BATCH = 32
DEC_SEQ = 1
PAST_LEN = 8192
PAGE_SIZE = 128

N_MIXERS = 4
GROUP_WIDTH = D_MODEL // N_MIXERS
HEAD_DIM = 64
N_HEADS = GROUP_WIDTH // HEAD_DIM
CMP_BLOCK = 32
SEL_BLOCK = 64
SEL_TOPK = 16
WINDOW = 512
FORCE_SCORE = 1.0e4
S5_GROUP = 16
S5_GROUPS = GROUP_WIDTH // S5_GROUP
S5_STATE = 64
CONV_W = 4
GDN_CHUNK = 64
FOX_BIAS_INIT = 3.0
N_EXPERTS = 16
N_EXPERT_GROUPS = 4
EXPERTS_PER_GROUP = N_EXPERTS // N_EXPERT_GROUPS
TOP_K = 2
D_EXPERT = D_MODEL // 4
Q_BLOCK = 128
ALPHA = (2.0 * DEPTH) ** 0.25
BETA_INIT = (8.0 * DEPTH) ** -0.25
LN_EPS = 1e-5
RMS_EPS = 1e-6
COL_WIDTHS = (GROUP_WIDTH, HEAD_DIM, HEAD_DIM, HEAD_DIM, HEAD_DIM, HEAD_DIM, HEAD_DIM, 3 * N_HEADS,
              GROUP_WIDTH,
              GROUP_WIDTH, GROUP_WIDTH, GROUP_WIDTH, N_HEADS, N_HEADS, GROUP_WIDTH,
              GROUP_WIDTH, GROUP_WIDTH, GROUP_WIDTH, N_HEADS)
IN_COLS = sum(COL_WIDTHS)

kernel_name = 'hybrid_nsa_s5_gdn_fox_moe_step'


def layer_norm(x, g, b):
    xf = x.astype(jnp.float32)
    mu = jnp.mean(xf, axis=-1, keepdims=True)
    var = jnp.mean(jnp.square(xf - mu), axis=-1, keepdims=True)
    y = (xf - mu) * lax.rsqrt(var + LN_EPS) * g.astype(jnp.float32) + b.astype(jnp.float32)
    return y.astype(x.dtype)


def rms_norm(x, g):
    xf = x.astype(jnp.float32)
    y = xf * lax.rsqrt(jnp.mean(xf * xf, axis=-1, keepdims=True) + RMS_EPS) * g.astype(jnp.float32)
    return y.astype(x.dtype)


def l2_normalize(x):
    return x * lax.rsqrt(jnp.sum(x * x, axis=-1, keepdims=True) + RMS_EPS)


def masked_softmax(s, mask):
    s = jnp.where(mask, s.astype(jnp.float32), -jnp.inf)
    m = jnp.max(s, axis=-1, keepdims=True)
    m = jnp.where(jnp.isfinite(m), m, 0.0)
    p = jnp.exp(s - m)
    return p / jnp.maximum(jnp.sum(p, axis=-1, keepdims=True), 1e-30)


def alibi_slopes(n):
    return 2.0 ** (-8.0 * (jnp.arange(n, dtype=jnp.float32) + 1.0) / n)


def query_block(L):
    return L if L <= Q_BLOCK else math.gcd(L, Q_BLOCK)


def to_blocks(a, qb):
    B, L = a.shape[:2]
    return jnp.moveaxis(a.reshape((B, L // qb, qb) + a.shape[2:]), 1, 0)


def from_blocks(a):
    nqb, B, qb = a.shape[:3]
    return jnp.moveaxis(a, 0, 1).reshape((B, nqb * qb) + a.shape[3:])


def gather_pages(pool, page_table):
    rows = pool[page_table]
    return rows.reshape((page_table.shape[0], page_table.shape[1] * pool.shape[1]) + pool.shape[2:])


def split_points():
    pts, acc = [], 0
    for w in COL_WIDTHS[:-1]:
        acc += w
        pts.append(acc)
    return pts


def nsa_mixer(q, kv_all, win_ext, gate_logits, pool_w, p0, w_pad):
    f32 = jnp.float32
    B, L, H, hd = q.shape
    T = kv_all.shape[1]
    scale = hd ** -0.5
    slopes = alibi_slopes(H)
    t_pos = p0 + jnp.arange(L)
    t_pad = -(-T // SEL_BLOCK) * SEL_BLOCK
    kv_all = jnp.pad(kv_all, ((0, 0), (0, t_pad - T), (0, 0), (0, 0)))
    n_cmp, n_sel = t_pad // CMP_BLOCK, t_pad // SEL_BLOCK
    cmp_rows = kv_all[:, :, 0:2].reshape(B, n_cmp, CMP_BLOCK, 2, hd)
    cmp_kv = jnp.einsum('bnjcd,cj->bncd', cmp_rows, pool_w.astype(cmp_rows.dtype))
    k_c, v_c = cmp_kv[:, :, 0], cmp_kv[:, :, 1]
    cmp_end = (jnp.arange(n_cmp) + 1) * CMP_BLOCK - 1
    dist_c = t_pos[:, None] - cmp_end[None, :]
    s_c = jnp.einsum('blhd,bnd->bhln', q, k_c).astype(f32) * scale - slopes[:, None, None] * dist_c
    p_c = masked_softmax(s_c, dist_c >= 0)
    o_c = jnp.einsum('bhln,bnd->blhd', p_c.astype(q.dtype), v_c)
    imp = jnp.sum(p_c, axis=1).reshape(B, L, n_sel, SEL_BLOCK // CMP_BLOCK).sum(-1)
    blk = jnp.arange(n_sel)
    cur = t_pos // SEL_BLOCK
    forced = (blk[None, :] == cur[:, None]) | (blk[None, :] == 0)
    valid = blk[None, :] <= cur[:, None]
    score = jnp.where(valid, jnp.where(forced, FORCE_SCORE, imp), -jnp.inf)
    k_top = min(SEL_TOPK, n_sel)
    top_val, top_idx = lax.top_k(score, k_top)
    top_ok = jnp.isfinite(top_val)
    sel_rows = kv_all[:, :, 2:4].reshape(B, n_sel, SEL_BLOCK, 2, hd)
    qb = query_block(L)

    def one_block(args):
        i, q_i, idx_i, ok_i, g_i, oc_i = args
        t_i = p0 + i * qb + jnp.arange(qb)
        rows = jax.vmap(lambda r, ix: r[ix])(sel_rows, idx_i)
        rows = rows.reshape(B, qb, k_top * SEL_BLOCK, 2, hd)
        key_pos = (idx_i[..., None] * SEL_BLOCK + jnp.arange(SEL_BLOCK)).reshape(B, qb, k_top * SEL_BLOCK)
        dist_s = t_i[None, :, None] - key_pos
        mask_s = (dist_s >= 0) & jnp.repeat(ok_i, SEL_BLOCK, axis=-1)
        s_s = (jnp.einsum('bqhd,bqkd->bhqk', q_i, rows[..., 0, :]).astype(f32) * scale
               - slopes[None, :, None, None] * dist_s[:, None])
        p_s = masked_softmax(s_s, mask_s[:, None])
        o_s = jnp.einsum('bhqk,bqkd->bqhd', p_s.astype(q_i.dtype), rows[..., 1, :])
        win = lax.dynamic_slice_in_dim(win_ext, i * qb, w_pad + qb, axis=1)
        w_pos = p0 - w_pad + i * qb + jnp.arange(w_pad + qb)
        dist_w = t_i[:, None] - w_pos[None, :]
        mask_w = (dist_w >= 0) & (dist_w < WINDOW) & (w_pos[None, :] >= 0)
        s_w = (jnp.einsum('bqhd,bkd->bhqk', q_i, win[:, :, 0]).astype(f32) * scale
               - slopes[:, None, None] * dist_w)
        p_w = masked_softmax(s_w, mask_w)
        o_w = jnp.einsum('bhqk,bkd->bqhd', p_w.astype(q_i.dtype), win[:, :, 1])
        g = jax.nn.sigmoid(g_i.astype(f32))
        out = g[..., 0:1] * oc_i + g[..., 1:2] * o_s + g[..., 2:3] * o_w
        return out.astype(q_i.dtype)

    xs = (jnp.arange(L // qb), to_blocks(q, qb), to_blocks(top_idx, qb), to_blocks(top_ok, qb),
          to_blocks(gate_logits, qb), to_blocks(o_c, qb))
    out = from_blocks(lax.map(one_block, xs))
    return out.reshape(B, L, H * hd)


def s5_mixer(u, a_re, a_im, b_re, b_im, c_re, c_im, d, log_dt, w_glu, s0_re, s0_im):
    f32 = jnp.float32
    B, L, W = u.shape
    uf = u.astype(f32)
    ug = uf.reshape(B, L, S5_GROUPS, S5_GROUP)
    ar, ai = a_re.astype(f32), a_im.astype(f32)
    dt = jnp.exp(log_dt.astype(f32))[:, None]
    mag = jnp.exp(dt * ar)
    abar_re, abar_im = mag * jnp.cos(dt * ai), mag * jnp.sin(dt * ai)
    den = ar * ar + ai * ai
    num_re, num_im = abar_re - 1.0, abar_im
    zoh_re = (num_re * ar + num_im * ai) / den
    zoh_im = (num_im * ar - num_re * ai) / den
    br, bi = b_re.astype(f32), b_im.astype(f32)
    bbar_re = zoh_re[..., None] * br - zoh_im[..., None] * bi
    bbar_im = zoh_re[..., None] * bi + zoh_im[..., None] * br
    x_re = jnp.einsum('blgc,gnc->blgn', ug, bbar_re)
    x_im = jnp.einsum('blgc,gnc->blgn', ug, bbar_im)
    s0r, s0i = s0_re.astype(f32), s0_im.astype(f32)
    x_re = x_re.at[:, 0].add(abar_re * s0r - abar_im * s0i)
    x_im = x_im.at[:, 0].add(abar_re * s0i + abar_im * s0r)
    ea_re = jnp.broadcast_to(abar_re, x_re.shape)
    ea_im = jnp.broadcast_to(abar_im, x_im.shape)

    def combine(e1, e2):
        ar1, ai1, br1, bi1 = e1
        ar2, ai2, br2, bi2 = e2
        return (ar2 * ar1 - ai2 * ai1, ar2 * ai1 + ai2 * ar1,
                ar2 * br1 - ai2 * bi1 + br2, ar2 * bi1 + ai2 * br1 + bi2)

    _, _, s_re, s_im = lax.associative_scan(combine, (ea_re, ea_im, x_re, x_im), axis=1)
    cr, ci = c_re.astype(f32), c_im.astype(f32)
    y = jnp.einsum('blgn,gcn->blgc', s_re, cr) - jnp.einsum('blgn,gcn->blgc', s_im, ci)
    y = jax.nn.gelu(y.reshape(B, L, W) + d.astype(f32) * uf)
    out = y * jax.nn.sigmoid(jnp.einsum('blc,ce->ble', y, w_glu.astype(f32)))
    return out, s_re[:, -1], s_im[:, -1]


def gated_delta_chunked(q, k, v, g, beta, s0):
    B, L, H, dk = q.shape
    dv = v.shape[-1]
    C = min(GDN_CHUNK, L)
    n = -(-L // C)
    pad = n * C - L

    def prep(a):
        a = jnp.pad(a, ((0, 0), (0, pad)) + ((0, 0),) * (a.ndim - 2))
        a = a.reshape((B, n, C) + a.shape[2:])
        return jnp.moveaxis(jnp.moveaxis(a, 1, 0), 3, 2)

    tril = jnp.tril(jnp.ones((C, C), bool))
    strict = jnp.tril(jnp.ones((C, C), bool), -1)
    eye = jnp.eye(C, dtype=jnp.float32)

    def chunk_step(S, xs):
        qc, kc, vc, gc, bc = xs
        G = jnp.cumsum(gc, axis=-1)
        decay = jnp.exp(jnp.where(tril, G[..., :, None] - G[..., None, :], -jnp.inf))
        kb = kc * bc[..., None]
        M = jnp.where(strict, jnp.einsum('bhid,bhjd->bhij', kb, kc) * decay, 0.0)
        Tinv = lax.linalg.triangular_solve(eye + M, jnp.broadcast_to(eye, M.shape),
                                           left_side=True, lower=True, unit_diagonal=True)
        u = Tinv @ (vc * bc[..., None])
        w = Tinv @ (kb * jnp.exp(G)[..., None])
        v_new = u - w @ S
        attn = jnp.einsum('bhid,bhjd->bhij', qc, kc) * decay
        o = (qc * jnp.exp(G)[..., None]) @ S + attn @ v_new
        g_last = G[..., -1:]
        S_new = (S * jnp.exp(g_last)[..., None]
                 + jnp.einsum('bhjd,bhje->bhde', kc * jnp.exp(g_last - G)[..., None], v_new))
        return S_new, o

    S_fin, o = lax.scan(chunk_step, s0, (prep(q), prep(k), prep(v), prep(g), prep(beta)))
    o = jnp.moveaxis(jnp.moveaxis(o, 2, 3), 0, 1).reshape(B, n * C, H, dv)[:, :L]
    return o, S_fin


def gdn_mixer(q, k, v, a, b, z, conv_w, a_log, dt_bias, norm_g, conv_buf, s0):
    f32 = jnp.float32
    B, L, _ = q.shape
    H, hd = N_HEADS, HEAD_DIM
    qkv = jnp.concatenate([q, k, v], axis=-1)
    ext = jnp.concatenate([conv_buf.astype(qkv.dtype), qkv], axis=1)
    conv = ext[:, 0:L] * conv_w[0]
    for j in range(1, CONV_W):
        conv = conv + ext[:, j:j + L] * conv_w[j]
    conv = jax.nn.silu(conv.astype(f32))
    qc, kc, vc = jnp.split(conv, 3, axis=-1)
    qh = l2_normalize(qc.reshape(B, L, H, hd)) * hd ** -0.5
    kh = l2_normalize(kc.reshape(B, L, H, hd))
    vh = vc.reshape(B, L, H, hd)
    g = -jnp.exp(a_log.astype(f32)) * jax.nn.softplus(a.astype(f32) + dt_bias.astype(f32))
    beta = jax.nn.sigmoid(b.astype(f32))
    o, s_fin = gated_delta_chunked(qh, kh, vh, g, beta, s0.astype(f32))
    o = rms_norm(o, norm_g) * jax.nn.silu(z.astype(f32).reshape(B, L, H, hd))
    return o.reshape(B, L, H * hd), s_fin, ext[:, L:]


def fox_attention(q, k, v, logf_all, p0):
    f32 = jnp.float32
    B, L, H, hd = q.shape
    T = k.shape[1]
    scale = hd ** -0.5
    cum = jnp.cumsum(logf_all.astype(f32), axis=1)
    cum_k = jnp.transpose(cum, (0, 2, 1))
    cum_q = cum[:, p0:]
    key_pos = jnp.arange(T)
    qb = query_block(L)

    def one_block(args):
        i, q_i, cq_i = args
        t_i = p0 + i * qb + jnp.arange(qb)
        s = jnp.einsum('bqhd,bkhd->bhqk', q_i, k).astype(f32) * scale
        s = s + jnp.transpose(cq_i, (0, 2, 1))[..., None] - cum_k[:, :, None, :]
        p = masked_softmax(s, key_pos[None, :] <= t_i[:, None])
        return jnp.einsum('bhqk,bkhd->bqhd', p.astype(v.dtype), v)

    out = lax.map(one_block, (jnp.arange(L // qb), to_blocks(q, qb), to_blocks(cum_q, qb)))
    return from_blocks(out).reshape(B, L, H * hd)


def token_mixers(x, lp, nsa_past, fox_kv_past, fox_logf_past, win_buf, win_keep, gdn_s0, conv_buf, s5_re0, s5_im0):
    f32 = jnp.float32
    B, L, _ = x.shape
    H, hd = N_HEADS, HEAD_DIM
    p0 = nsa_past.shape[1]
    w_pad = win_buf.shape[1]
    proj = jnp.einsum('bld,dc->blc', x, lp['w_in'])
    (nq, nkc, nvc, nks, nvs, nkw, nvw, ngate, s5u,
     gq, gk, gv, ga, gb, gz, fq, fk, fv, ff) = jnp.split(proj, split_points(), axis=-1)
    nsa_new = jnp.stack([nkc, nvc, nks, nvs], axis=2)
    nsa_all = jnp.concatenate([nsa_past.astype(x.dtype), nsa_new], axis=1)
    win_ext = jnp.concatenate([win_buf.astype(x.dtype), jnp.stack([nkw, nvw], axis=2)], axis=1)
    o_nsa = nsa_mixer(nq.reshape(B, L, H, hd), nsa_all, win_ext, ngate.reshape(B, L, H, 3),
                      lp['nsa_pool'], p0, w_pad)
    win_state = win_ext[:, win_ext.shape[1] - win_keep:]
    o_s5, s5_re, s5_im = s5_mixer(s5u, lp['s5_a_re'], lp['s5_a_im'], lp['s5_b_re'], lp['s5_b_im'],
                                  lp['s5_c_re'], lp['s5_c_im'], lp['s5_d'], lp['s5_log_dt'], lp['s5_w_glu'],
                                  s5_re0, s5_im0)
    g_nsa, g_s5, g_gdn, g_fox = jnp.split(lp['mix_norm'], N_MIXERS)
    o_gdn, gdn_state, conv_state = gdn_mixer(gq, gk, gv, ga, gb, gz, lp['gdn_conv'], lp['gdn_a_log'],
                                             lp['gdn_dt_bias'], g_gdn.reshape(H, hd), conv_buf, gdn_s0)
    logf_new = jax.nn.log_sigmoid(ff.astype(f32) + lp['fox_b_f'].astype(f32))
    fox_new = jnp.stack([fk.reshape(B, L, H, hd), fv.reshape(B, L, H, hd)], axis=2)
    fox_all = jnp.concatenate([fox_kv_past.astype(x.dtype), fox_new], axis=1)
    logf_all = jnp.concatenate([fox_logf_past.astype(f32), logf_new], axis=1)
    o_fox = fox_attention(fq.reshape(B, L, H, hd), fox_all[:, :, 0], fox_all[:, :, 1], logf_all, p0)
    mixed = jnp.concatenate([rms_norm(o_nsa, g_nsa), rms_norm(o_s5, g_s5).astype(x.dtype),
                             o_gdn.astype(x.dtype), rms_norm(o_fox, g_fox)], axis=-1)
    y = jnp.einsum('blc,cd->bld', mixed.astype(x.dtype), lp['w_out']).astype(x.dtype)
    return y, (nsa_new, fox_new, logf_new, win_state, gdn_state, conv_state, s5_re, s5_im)


def moe_ffn(x, router_w, router_b, w1, w3, w2):
    f32 = jnp.float32
    B, L, _ = x.shape
    logits = jnp.einsum('bld,de->ble', x, router_w).astype(f32) + router_b.astype(f32)
    probs = jax.nn.softmax(logits, axis=-1)
    pg = probs.reshape(B, L, N_EXPERT_GROUPS, EXPERTS_PER_GROUP)
    group_score = jnp.sum(lax.top_k(pg, TOP_K)[0], axis=-1)
    grp = jnp.argmax(group_score, axis=-1)
    in_grp = jnp.take_along_axis(pg, grp[:, :, None, None], axis=2)[:, :, 0]
    top_p, top_i = lax.top_k(in_grp, TOP_K)
    weights = top_p / jnp.sum(top_p, axis=-1, keepdims=True)
    chosen = grp[..., None] * EXPERTS_PER_GROUP + top_i
    gate = jnp.sum(jax.nn.one_hot(chosen, N_EXPERTS, dtype=f32) * weights[..., None], axis=-2)
    h = jax.nn.silu(jnp.einsum('bld,edf->blef', x, w1)) * jnp.einsum('bld,edf->blef', x, w3)
    h = h * gate.astype(h.dtype)[..., None]
    y = jnp.einsum('blef,efd->bld', h, w2)
    return y.astype(x.dtype)


def setup_inputs(seed: int = 0) -> dict:
    key = jax.random.key(seed)
    keys = iter(jax.random.split(key, 48))
    f32 = jnp.float32
    n_pages = PAST_LEN // PAGE_SIZE
    n_pool = (DEC_BATCH * n_pages * 5) // 4
    win_buf = min(WINDOW, PAST_LEN)
    GW, H, hd = GROUP_WIDTH, N_HEADS, HEAD_DIM

    def nrm(shape, scale=1.0):
        return scale * jax.random.normal(next(keys), shape, f32)

    def unif(shape, lo, hi):
        return jax.random.uniform(next(keys), shape, f32, lo, hi)

    x_prompt = nrm((BATCH, SEQ, D_MODEL))
    x_sample = nrm((DEC_BATCH, DEC_SEQ, D_MODEL))
    cache_nsa_kv = nrm((DEPTH, n_pool, PAGE_SIZE, 4, hd))
    cache_fox_kv = nrm((DEPTH, n_pool, PAGE_SIZE, 2, H, hd))
    cache_fox_logf = jax.nn.log_sigmoid(FOX_BIAS_INIT + nrm((DEPTH, n_pool, PAGE_SIZE, H)))
    state_nsa_win = nrm((DEPTH, DEC_BATCH, win_buf, 2, hd))
    state_gdn = nrm((DEPTH, DEC_BATCH, H, hd, hd), 0.1)
    state_gdn_conv = nrm((DEPTH, DEC_BATCH, CONV_W - 1, 3 * GW))
    state_s5_re = nrm((DEPTH, DEC_BATCH, S5_GROUPS, S5_STATE), 0.1)
    state_s5_im = nrm((DEPTH, DEC_BATCH, S5_GROUPS, S5_STATE), 0.1)
    page_table = jax.random.permutation(next(keys), n_pool)[: DEC_BATCH * n_pages]
    page_table = page_table.reshape(DEC_BATCH, n_pages).astype(jnp.int32)

    w_in = nrm((DEPTH, D_MODEL, IN_COLS), D_MODEL ** -0.5)
    nsa_pool = (1.0 + nrm((DEPTH, 2, CMP_BLOCK), 0.1)) / CMP_BLOCK
    s5_a_re = -0.5 + nrm((DEPTH, S5_GROUPS, S5_STATE), 0.01)
    s5_a_im = jnp.pi * jnp.arange(S5_STATE, dtype=f32) + nrm((DEPTH, S5_GROUPS, S5_STATE), 0.01)
    s5_b_re = nrm((DEPTH, S5_GROUPS, S5_STATE, S5_GROUP), (2.0 * S5_GROUP) ** -0.5)
    s5_b_im = nrm((DEPTH, S5_GROUPS, S5_STATE, S5_GROUP), (2.0 * S5_GROUP) ** -0.5)
    s5_c_re = nrm((DEPTH, S5_GROUPS, S5_GROUP, S5_STATE), (2.0 * S5_STATE) ** -0.5)
    s5_c_im = nrm((DEPTH, S5_GROUPS, S5_GROUP, S5_STATE), (2.0 * S5_STATE) ** -0.5)
    s5_d = nrm((DEPTH, GW))
    s5_log_dt = unif((DEPTH, S5_GROUPS), math.log(1e-3), math.log(1e-1))
    s5_w_glu = nrm((DEPTH, GW, GW), GW ** -0.5)
    gdn_conv = nrm((DEPTH, CONV_W, 3 * GW), CONV_W ** -0.5)
    gdn_a_log = jnp.log(unif((DEPTH, H), 1.0, 16.0))
    dt = jnp.exp(unif((DEPTH, H), math.log(1e-3), math.log(1e-1)))
    gdn_dt_bias = dt + jnp.log(-jnp.expm1(-dt))
    fox_b_f = FOX_BIAS_INIT + nrm((DEPTH, H), 0.1)
    mix_norm = 1.0 + nrm((DEPTH, D_MODEL), 0.02)
    w_out = nrm((DEPTH, D_MODEL, D_MODEL), BETA_INIT * D_MODEL ** -0.5)
    ln1_g = 1.0 + nrm((DEPTH, D_MODEL), 0.02)
    ln1_b = nrm((DEPTH, D_MODEL), 0.02)
    ln2_g = 1.0 + nrm((DEPTH, D_MODEL), 0.02)
    ln2_b = nrm((DEPTH, D_MODEL), 0.02)
    router_w = nrm((D_MODEL, N_EXPERTS), D_MODEL ** -0.5)
    router_b = nrm((N_EXPERTS,), 0.01)
    exp_w1 = nrm((DEPTH, N_EXPERTS, D_MODEL, D_EXPERT), D_MODEL ** -0.5)
    exp_w3 = nrm((DEPTH, N_EXPERTS, D_MODEL, D_EXPERT), D_MODEL ** -0.5)
    exp_w2 = nrm((DEPTH, N_EXPERTS, D_EXPERT, D_MODEL), BETA_INIT * D_EXPERT ** -0.5)
    return {'x_prompt': x_prompt, 'x_sample': x_sample,
            'cache_nsa_kv': cache_nsa_kv, 'cache_fox_kv': cache_fox_kv, 'cache_fox_logf': cache_fox_logf,
            'state_nsa_win': state_nsa_win, 'state_gdn': state_gdn, 'state_gdn_conv': state_gdn_conv,
            'state_s5_re': state_s5_re, 'state_s5_im': state_s5_im, 'page_table': page_table,
            'w_in': w_in, 'nsa_pool': nsa_pool, 's5_a_re': s5_a_re, 's5_a_im': s5_a_im,
            's5_b_re': s5_b_re, 's5_b_im': s5_b_im, 's5_c_re': s5_c_re, 's5_c_im': s5_c_im,
            's5_d': s5_d, 's5_log_dt': s5_log_dt, 's5_w_glu': s5_w_glu,
            'gdn_conv': gdn_conv, 'gdn_a_log': gdn_a_log, 'gdn_dt_bias': gdn_dt_bias, 'fox_b_f': fox_b_f,
            'mix_norm': mix_norm, 'w_out': w_out, 'ln1_g': ln1_g, 'ln1_b': ln1_b, 'ln2_g': ln2_g, 'ln2_b': ln2_b,
            'router_w': router_w, 'router_b': router_b, 'exp_w1': exp_w1, 'exp_w3': exp_w3, 'exp_w2': exp_w2}


def reference(x_prompt, x_sample, cache_nsa_kv, cache_fox_kv, cache_fox_logf, state_nsa_win, state_gdn,
              state_gdn_conv, state_s5_re, state_s5_im, page_table, w_in, nsa_pool, s5_a_re, s5_a_im,
              s5_b_re, s5_b_im, s5_c_re, s5_c_im, s5_d, s5_log_dt, s5_w_glu, gdn_conv, gdn_a_log, gdn_dt_bias,
              fox_b_f, mix_norm, w_out, ln1_g, ln1_b, ln2_g, ln2_b, router_w, router_b, exp_w1, exp_w3, exp_w2):
    f32 = jnp.float32
    B, S = x_prompt.shape[:2]
    H, hd = N_HEADS, HEAD_DIM
    win_keep_p = min(WINDOW, S)
    win_keep_s = state_nsa_win.shape[2]
    acc_p = [[] for _ in range(8)]
    acc_s = [[] for _ in range(8)]
    xp, xs = x_prompt, x_sample
    for l in range(DEPTH):
        lp = {'w_in': w_in[l], 'nsa_pool': nsa_pool[l], 's5_a_re': s5_a_re[l], 's5_a_im': s5_a_im[l],
              's5_b_re': s5_b_re[l], 's5_b_im': s5_b_im[l], 's5_c_re': s5_c_re[l], 's5_c_im': s5_c_im[l],
              's5_d': s5_d[l], 's5_log_dt': s5_log_dt[l], 's5_w_glu': s5_w_glu[l], 'gdn_conv': gdn_conv[l],
              'gdn_a_log': gdn_a_log[l], 'gdn_dt_bias': gdn_dt_bias[l], 'fox_b_f': fox_b_f[l],
              'mix_norm': mix_norm[l], 'w_out': w_out[l]}
        y, st = token_mixers(xp, lp,
                             jnp.zeros((B, 0, 4, hd), xp.dtype), jnp.zeros((B, 0, 2, H, hd), xp.dtype),
                             jnp.zeros((B, 0, H), f32), jnp.zeros((B, WINDOW, 2, hd), xp.dtype), win_keep_p,
                             jnp.zeros((B, H, hd, hd), f32), jnp.zeros((B, CONV_W - 1, 3 * GROUP_WIDTH), xp.dtype),
                             jnp.zeros((B, S5_GROUPS, S5_STATE), f32), jnp.zeros((B, S5_GROUPS, S5_STATE), f32))
        xp = layer_norm(ALPHA * xp + y, ln1_g[l], ln1_b[l])
        xp = layer_norm(ALPHA * xp + moe_ffn(xp, router_w, router_b, exp_w1[l], exp_w3[l], exp_w2[l]),
                        ln2_g[l], ln2_b[l])
        for a, v in zip(acc_p, st):
            a.append(v)
        y, st = token_mixers(xs, lp,
                             gather_pages(cache_nsa_kv[l], page_table), gather_pages(cache_fox_kv[l], page_table),
                             gather_pages(cache_fox_logf[l], page_table), state_nsa_win[l], win_keep_s,
                             state_gdn[l], state_gdn_conv[l], state_s5_re[l], state_s5_im[l])
        xs = layer_norm(ALPHA * xs + y, ln1_g[l], ln1_b[l])
        xs = layer_norm(ALPHA * xs + moe_ffn(xs, router_w, router_b, exp_w1[l], exp_w3[l], exp_w2[l]),
                        ln2_g[l], ln2_b[l])
        for a, v in zip(acc_s, st):
            a.append(v)
    nsa_rows_p, fox_kv_p, fox_logf_p, nsa_win_p, gdn_p, gdn_conv_p, s5_re_p, s5_im_p = [jnp.stack(a) for a in acc_p]
    nsa_rows_s, fox_kv_s, fox_logf_s, nsa_win_s, gdn_s, gdn_conv_s, s5_re_s, s5_im_s = [jnp.stack(a) for a in acc_s]
    y_prompt, y_sample = xp, xs
    return (y_prompt, y_sample, nsa_rows_p, nsa_rows_s, fox_kv_p, fox_kv_s, fox_logf_p, fox_logf_s,
            nsa_win_p, nsa_win_s, gdn_p, gdn_s, gdn_conv_p, gdn_conv_s, s5_re_p, s5_re_s, s5_im_p, s5_im_s)
```

```python
import functools
import math

import numpy as np
import jax
import jax.numpy as jnp
from jax import lax
from jax.experimental import pallas as pl
from jax.experimental.pallas import tpu as pltpu

F32 = jnp.float32
BF16 = jnp.bfloat16
HI = lax.Precision.HIGHEST

DEPTH = 4
N_MIXERS = 4
GROUP_WIDTH = 256
HEAD_DIM = 64
N_HEADS = 4
CMP_BLOCK = 32
SEL_BLOCK = 64
SEL_TOPK = 16
WINDOW = 512
FORCE_SCORE = 1.0e4
S5_GROUP = 16
S5_GROUPS = 16
S5_STATE = 64
CONV_W = 4
GDN_CHUNK = 64
N_EXPERTS = 16
N_EXPERT_GROUPS = 4
EXPERTS_PER_GROUP = 4
D_EXPERT = 256
ALPHA = (2.0 * DEPTH) ** 0.25
LN_EPS = 1e-5
RMS_EPS = 1e-6
NEG = -1e30

VMEM_LIMIT = 56 * 1024 * 1024

C_GQKV, C_NQ, C_FKV, C_NROWS, C_S5U, C_GZ, C_FQ, C_WIN, C_SMALL = 0, 768, 1024, 1536, 1792, 2048, 2304, 2560, 2688
PROJ_COLS = 2816
S_NGATE, S_GA, S_GB, S_FF = 0, 12, 16, 20


_IN_PROJ_SEGS = ((908, 1676), (0, 256), (2196, 2708), (256, 512), (652, 908), (1684, 1940), (1940, 2196),
                 (512, 640), (640, 652), (1676, 1680), (1680, 1684), (2708, 2712))


def _in_proj_perm():
    return np.concatenate([np.arange(a, b) for a, b in _IN_PROJ_SEGS])


def _permute_w_in(w):
    used = sum(b - a for a, b in _IN_PROJ_SEGS)
    parts = [w[:, a:b] for a, b in _IN_PROJ_SEGS] + [jnp.zeros((w.shape[0], PROJ_COLS - used), w.dtype)]
    return jnp.concatenate(parts, axis=1)


def _dotw(a, w):
    if w.dtype == BF16:
        return jnp.dot(a.astype(BF16), w, preferred_element_type=F32)
    return jnp.dot(a, w, precision=HI, preferred_element_type=F32)


def _cparams(n_axes):
    return pltpu.CompilerParams(dimension_semantics=("arbitrary",) * n_axes, vmem_limit_bytes=VMEM_LIMIT)


def _resident(shape):
    nd = len(shape)
    return pl.BlockSpec(shape, lambda *_: (0,) * nd, pipeline_mode=pl.Buffered(1))


def _dot_nt(a, b):
    return lax.dot_general(a, b, (((1,), (1,)), ((), ())), preferred_element_type=F32)


def _dot_tn(a, b):
    return lax.dot_general(a, b, (((0,), (0,)), ((), ())), preferred_element_type=F32)


def _rms(x, g):
    return x * lax.rsqrt(jnp.mean(x * x, axis=-1, keepdims=True) + RMS_EPS) * g


def _ln(x, g, b):
    mu = jnp.mean(x, axis=-1, keepdims=True)
    xc = x - mu
    var = jnp.mean(xc * xc, axis=-1, keepdims=True)
    return xc * lax.rsqrt(var + LN_EPS) * g + b


def _in_proj_body(x_ref, w_ref, o_ref, *, nch):
    x = x_ref[...]
    x = x.astype(BF16) if w_ref.dtype == BF16 else x
    for j in range(0, o_ref.shape[1], nch):
        o_ref[:, j:j + nch] = _dotw(x, w_ref[:, j:j + nch])


def _in_proj(x2d, w, tm):
    m, k = x2d.shape
    n = w.shape[1]
    return pl.pallas_call(
        functools.partial(_in_proj_body, nch=256),
        grid=(m // tm,),
        in_specs=[pl.BlockSpec((tm, k), lambda i: (i, 0)), _resident((k, n))],
        out_specs=pl.BlockSpec((tm, n), lambda i: (i, 0)),
        out_shape=jax.ShapeDtypeStruct((m, n), F32),
        compiler_params=_cparams(1),
        name="in_proj",
    )(x2d, w)


def _out_proj_body(x_ref, o0_ref, o1_ref, o2_ref, o3_ref, w_ref, g_ref, b_ref, out_ref):
    gw = GROUP_WIDTH
    y = _dotw(o0_ref[...], w_ref[0:gw, :])
    y = y + _dotw(o1_ref[...], w_ref[gw:2 * gw, :])
    y = y + _dotw(o2_ref[...], w_ref[2 * gw:3 * gw, :])
    y = y + _dotw(o3_ref[...], w_ref[3 * gw:4 * gw, :])
    out_ref[...] = _ln(ALPHA * x_ref[...] + y, g_ref[...], b_ref[...])


def _out_proj_ln(x2d, mixers, w_out, g, b, tm):
    m, d = x2d.shape
    gw = GROUP_WIDTH
    row = lambda i: (i, 0)
    return pl.pallas_call(
        _out_proj_body,
        grid=(m // tm,),
        in_specs=[pl.BlockSpec((tm, d), row)] + [pl.BlockSpec((tm, gw), row)] * 4
        + [_resident((d, d)), _resident((1, d)), _resident((1, d))],
        out_specs=pl.BlockSpec((tm, d), row),
        out_shape=jax.ShapeDtypeStruct((m, d), F32),
        compiler_params=_cparams(1),
        name="out_proj_ln",
    )(x2d, *mixers, w_out, g, b)


def _moe_gate_t(lt):
    m = jnp.max(lt, axis=0, keepdims=True)
    p = jnp.exp(lt - m)
    probs = p / jnp.sum(p, axis=0, keepdims=True)
    rows = [probs[e:e + 1, :] for e in range(N_EXPERTS)]
    n = EXPERTS_PER_GROUP
    scores = []
    for g in range(N_EXPERT_GROUPS):
        r = rows[g * n:(g + 1) * n]
        best = None
        for i in range(n):
            for j in range(i + 1, n):
                s = r[i] + r[j]
                best = s if best is None else jnp.maximum(best, s)
        scores.append(best)
    grp = jnp.zeros_like(scores[0], dtype=jnp.int32)
    top = scores[0]
    for g in range(1, N_EXPERT_GROUPS):
        take = scores[g] > top
        grp = jnp.where(take, g, grp)
        top = jnp.where(take, scores[g], top)
    vals = []
    for j in range(n):
        v = rows[j]
        for g in range(1, N_EXPERT_GROUPS):
            v = jnp.where(grp == g, rows[g * n + j], v)
        vals.append(v)

    def first_argmax(vs):
        idx = jnp.zeros_like(grp)
        best = vs[0]
        for j in range(1, n):
            take = vs[j] > best
            idx = jnp.where(take, j, idx)
            best = jnp.where(take, vs[j], best)
        return best, idx

    v1, i1 = first_argmax(vals)
    v2, i2 = first_argmax([jnp.where(i1 == j, -jnp.inf, vals[j]) for j in range(n)])
    tot = v1 + v2
    w1, w2 = v1 / tot, v2 / tot
    e1, e2 = grp * n + i1, grp * n + i2
    gate = [jnp.where(e1 == e, w1, 0.0) + jnp.where(e2 == e, w2, 0.0) for e in range(N_EXPERTS)]
    return jnp.concatenate(gate, axis=0)


def _moe_body(x_ref, rw_ref, rb_ref, w1_ref, w3_ref, w2_ref, g_ref, b_ref, out_ref, *, epc):
    x = x_ref[...]
    tm = x.shape[0]
    logits = jnp.dot(x, rw_ref[...], precision=HI, preferred_element_type=F32) + rb_ref[...]
    gate = _moe_gate_t(logits.T).T
    xb = x.astype(BF16)
    cw = epc * D_EXPERT
    y = jnp.zeros((tm, x.shape[1]), F32)
    for c in range(N_EXPERTS // epc):
        a = jnp.dot(xb, w1_ref[:, c * cw:(c + 1) * cw], preferred_element_type=F32)
        b = jnp.dot(xb, w3_ref[:, c * cw:(c + 1) * cw], preferred_element_type=F32)
        ge = jnp.concatenate(
            [jnp.broadcast_to(gate[:, e:e + 1], (tm, D_EXPERT)) for e in range(c * epc, (c + 1) * epc)], axis=1)
        h = (a * jax.nn.sigmoid(a)) * b * ge
        y = y + jnp.dot(h.astype(BF16), w2_ref[c * cw:(c + 1) * cw, :], preferred_element_type=F32)
    out_ref[...] = _ln(ALPHA * x + y, g_ref[...], b_ref[...])


def _moe_ln(x2d, rw, rb, w1, w3, w2, g, b, tm):
    m, d = x2d.shape
    ne = N_EXPERTS * D_EXPERT
    row = lambda i: (i, 0)
    return pl.pallas_call(
        functools.partial(_moe_body, epc=4),
        grid=(m // tm,),
        in_specs=[pl.BlockSpec((tm, d), row), _resident((d, N_EXPERTS)), _resident((1, N_EXPERTS)),
                  _resident((d, ne)), _resident((d, ne)), _resident((ne, d)), _resident((1, d)), _resident((1, d))],
        out_specs=pl.BlockSpec((tm, d), row),
        out_shape=jax.ShapeDtypeStruct((m, d), F32),
        compiler_params=_cparams(1),
        name="moe_ln",
    )(x2d, rw, rb, w1, w3, w2, g, b)


def _moe_dec_body(x_ref, rw_ref, rb_ref, w1_ref, w3_ref, w2_ref, g_ref, b_ref, out_ref, ge_ref, acc_ref, *, epc):
    c = pl.program_id(0)
    x = x_ref[...]
    tm = x.shape[0]

    @pl.when(c == 0)
    def _():
        logits = jnp.dot(x, rw_ref[...], precision=HI, preferred_element_type=F32) + rb_ref[...]
        gate = _moe_gate_t(logits.T).T
        for cc in range(N_EXPERTS // epc):
            ge_ref[cc] = jnp.concatenate(
                [jnp.broadcast_to(gate[:, e:e + 1], (tm, D_EXPERT)) for e in range(cc * epc, (cc + 1) * epc)], axis=1)
        acc_ref[...] = jnp.zeros(acc_ref.shape, F32)

    a = _dotw(x, w1_ref[...])
    b = _dotw(x, w3_ref[...])
    h = (a * jax.nn.sigmoid(a)) * b * ge_ref[c]
    acc_ref[...] += _dotw(h, w2_ref[...])

    @pl.when(c == pl.num_programs(0) - 1)
    def _():
        out_ref[...] = _ln(ALPHA * x + acc_ref[...], g_ref[...], b_ref[...])


def _moe_dec(x2d, rw, rb, w1, w3, w2, g, b, epc=2):
    m, d = x2d.shape
    cw = epc * D_EXPERT
    nchunk = N_EXPERTS // epc
    const = lambda c: (0, 0)
    return pl.pallas_call(
        functools.partial(_moe_dec_body, epc=epc),
        grid=(nchunk,),
        in_specs=[pl.BlockSpec((m, d), const), pl.BlockSpec((d, N_EXPERTS), const), pl.BlockSpec((1, N_EXPERTS), const),
                  pl.BlockSpec((d, cw), lambda c: (0, c)), pl.BlockSpec((d, cw), lambda c: (0, c)),
                  pl.BlockSpec((cw, d), lambda c: (c, 0)), pl.BlockSpec((1, d), const), pl.BlockSpec((1, d), const)],
        out_specs=pl.BlockSpec((m, d), const),
        out_shape=jax.ShapeDtypeStruct((m, d), F32),
        scratch_shapes=[pltpu.VMEM((nchunk, m, cw), F32), pltpu.VMEM((m, d), F32)],
        compiler_params=_cparams(1),
        name="moe_dec",
    )(x2d, rw, rb, w1, w3, w2, g, b)


def _fox_prep_body(ff_ref, bf_ref, logf_ref, cum_ref):
    lf = jax.nn.log_sigmoid(ff_ref[0] + bf_ref[...])
    logf_ref[0] = lf
    n = lf.shape[1]
    lane = lax.broadcasted_iota(jnp.int32, lf.shape, 1)
    c = lf
    s = 1
    while s < n:
        c = c + jnp.where(lane >= s, pltpu.roll(c, s, 1), 0.0)
        s *= 2
    cum_ref[0] = c


def _fox_prep(ff_t, bf):
    bsz, nh, n = ff_t.shape
    blk = pl.BlockSpec((1, nh, n), lambda b: (b, 0, 0))
    return pl.pallas_call(
        _fox_prep_body,
        grid=(bsz,),
        in_specs=[blk, _resident((nh, 1))],
        out_specs=[blk, blk],
        out_shape=[jax.ShapeDtypeStruct((bsz, nh, n), F32)] * 2,
        compiler_params=_cparams(1),
        name="fox_prep",
    )(ff_t, bf)


def _fox_body(q_ref, kv_ref, cq_ref, ck_ref, g_ref, o_ref, m_ref, l_ref, acc_ref, *, tq):
    qi = pl.program_id(1)
    hd, nh, gw = HEAD_DIM, N_HEADS, GROUP_WIDTH
    scale = hd ** -0.5
    q = q_ref[...]
    qb = [q[:, h * hd:(h + 1) * hd].astype(BF16) for h in range(nh)]
    cq = cq_ref[0]
    m_ref[...] = jnp.full(m_ref.shape, NEG, F32)
    l_ref[...] = jnp.zeros(l_ref.shape, F32)
    acc_ref[...] = jnp.zeros(acc_ref.shape, F32)
    row = lax.broadcasted_iota(jnp.int32, (tq, tq), 0)
    col = lax.broadcasted_iota(jnp.int32, (tq, tq), 1)

    def chunk(kc, diagonal):
        r0 = pl.multiple_of(kc * tq, tq)
        kv = kv_ref[pl.ds(r0, tq), :]
        ck = ck_ref[0, :, pl.ds(r0, tq)]
        for h in range(nh):
            k = kv[:, h * hd:(h + 1) * hd].astype(BF16)
            v = kv[:, gw + h * hd:gw + (h + 1) * hd].astype(BF16)
            s = _dot_nt(qb[h], k) * scale + cq[:, h:h + 1] - ck[h:h + 1, :]
            if diagonal:
                s = jnp.where(col <= row, s, NEG)
            m_old = m_ref[h]
            m_new = jnp.maximum(m_old, jnp.max(s, axis=1, keepdims=True))
            a = jnp.exp(m_old - m_new)
            p = jnp.exp(s - m_new)
            l_ref[h] = a * l_ref[h] + jnp.sum(p, axis=1, keepdims=True)
            acc_ref[:, h * hd:(h + 1) * hd] = (a * acc_ref[:, h * hd:(h + 1) * hd]
                                               + jnp.dot(p.astype(BF16), v, preferred_element_type=F32))
            m_ref[h] = m_new

    def body(kc, carry):
        chunk(kc, False)
        return carry

    lax.fori_loop(0, qi, body, 0)
    chunk(qi, True)
    o = jnp.concatenate([acc_ref[:, h * hd:(h + 1) * hd] / jnp.maximum(l_ref[h], 1e-30) for h in range(nh)], axis=1)
    o_ref[...] = _rms(o, g_ref[...])


def _fox_prompt(proj, cq, ck, g, bsz, n, tq):
    nh, gw = N_HEADS, GROUP_WIDTH
    nq = n // tq
    return pl.pallas_call(
        functools.partial(_fox_body, tq=tq),
        grid=(bsz, nq),
        in_specs=[pl.BlockSpec((tq, gw), lambda b, i: (b * nq + i, C_FQ // gw)),
                  pl.BlockSpec((n, 2 * gw), lambda b, i: (b, C_FKV // (2 * gw))),
                  pl.BlockSpec((1, tq, nh), lambda b, i: (b, i, 0)),
                  pl.BlockSpec((1, nh, n), lambda b, i: (b, 0, 0)),
                  _resident((1, gw))],
        out_specs=pl.BlockSpec((tq, gw), lambda b, i: (b * nq + i, 0)),
        out_shape=jax.ShapeDtypeStruct((bsz * n, gw), F32),
        scratch_shapes=[pltpu.VMEM((nh, tq, 1), F32), pltpu.VMEM((nh, tq, 1), F32), pltpu.VMEM((tq, gw), F32)],
        compiler_params=_cparams(2),
        name="fox_prompt",
    )(proj, proj, cq, ck, g)


def _slope_col(slopes_ref, tq):
    hrow = lax.broadcasted_iota(jnp.int32, (N_HEADS * tq, 1), 0) // tq
    s = jnp.full((N_HEADS * tq, 1), slopes_ref[0], F32)
    for h in range(1, N_HEADS):
        s = jnp.where(hrow == h, slopes_ref[h], s)
    return s


def _softmax_rows(s, mask):
    s = jnp.where(mask, s, NEG)
    m = jnp.max(s, axis=1, keepdims=True)
    p = jnp.where(mask, jnp.exp(s - m), 0.0)
    return p / jnp.maximum(jnp.sum(p, axis=1, keepdims=True), 1e-30)


def _nsa_body(slopes_ref, q_ref, rows_ref, win_ref, small_ref, pw_ref, pair_ref, exp_ref, g_ref, o_ref,
              kvc_ref, m_ref, l_ref, acc_ref, *, tq, tk, n):
    qi = pl.program_id(1)
    hd, nh = HEAD_DIM, N_HEADS
    scale = hd ** -0.5
    n_cmp, n_sel = n // CMP_BLOCK, n // SEL_BLOCK
    r4 = nh * tq

    @pl.when(qi == 0)
    def _():
        kv = rows_ref[:, 0:2 * hd].reshape(n_cmp, CMP_BLOCK, 2 * hd)
        kvc_ref[...] = jnp.sum(kv * pw_ref[...][None], axis=1)

    q = q_ref[...]
    qs = jnp.concatenate([q[:, h * hd:(h + 1) * hd] for h in range(nh)], axis=0).astype(BF16)
    slope = _slope_col(slopes_ref, tq)
    q0 = qi * tq
    t_row = q0 + lax.broadcasted_iota(jnp.int32, (r4, 1), 0) % tq

    kvc = kvc_ref[...]
    s_c = _dot_nt(qs, kvc[:, 0:hd].astype(BF16)) * scale
    cmp_end = (lax.broadcasted_iota(jnp.int32, (1, n_cmp), 1) + 1) * CMP_BLOCK - 1
    dist_c = t_row - cmp_end
    p_c = _softmax_rows(s_c - slope * dist_c.astype(F32), dist_c >= 0)
    o_c = jnp.dot(p_c.astype(BF16), kvc[:, hd:2 * hd].astype(BF16), preferred_element_type=F32)

    psum = p_c[0:tq]
    for h in range(1, nh):
        psum = psum + p_c[h * tq:(h + 1) * tq]
    imp = jnp.dot(psum, pair_ref[...], precision=HI, preferred_element_type=F32)
    blk = lax.broadcasted_iota(jnp.int32, (tq, n_sel), 1)
    cur = (q0 + lax.broadcasted_iota(jnp.int32, (tq, 1), 0)) // SEL_BLOCK
    valid = blk <= cur
    forced = (blk == cur) | (blk == 0)
    score = jnp.where(valid, jnp.where(forced, FORCE_SCORE, imp), -jnp.inf)
    rank = jnp.zeros((tq, n_sel), jnp.int32)
    for j in range(n_sel):
        sj = score[:, j:j + 1]
        beats = (sj > score) | ((sj == score) & (blk > j))
        rank = rank + beats.astype(jnp.int32)
    sel = ((rank < SEL_TOPK) & valid).astype(BF16)

    m_ref[...] = jnp.full(m_ref.shape, NEG, F32)
    l_ref[...] = jnp.zeros(l_ref.shape, F32)
    acc_ref[...] = jnp.zeros(acc_ref.shape, F32)
    lane_k = lax.broadcasted_iota(jnp.int32, (1, tk), 1)

    def body(kc, carry):
        r0 = pl.multiple_of(kc * tk, tk)
        kvs = rows_ref[pl.ds(r0, tk), 2 * hd:4 * hd]
        ks, vs = kvs[:, 0:hd].astype(BF16), kvs[:, hd:2 * hd].astype(BF16)
        dist = t_row - (r0 + lane_k)
        keep1 = jnp.dot(sel, exp_ref[kc], preferred_element_type=F32)
        keep = jnp.concatenate([keep1] * nh, axis=0)
        mask = (dist >= 0) & (keep > 0.5)
        s = _dot_nt(qs, ks) * scale - slope * dist.astype(F32)
        s = jnp.where(mask, s, NEG)
        m_old = m_ref[...]
        m_new = jnp.maximum(m_old, jnp.max(s, axis=1, keepdims=True))
        a = jnp.exp(m_old - m_new)
        p = jnp.where(mask, jnp.exp(s - m_new), 0.0)
        l_ref[...] = a * l_ref[...] + jnp.sum(p, axis=1, keepdims=True)
        acc_ref[...] = a * acc_ref[...] + jnp.dot(p.astype(BF16), vs, preferred_element_type=F32)
        m_ref[...] = m_new
        return carry

    lax.fori_loop(0, (q0 + tq + tk - 1) // tk, body, 0)
    o_s = acc_ref[...] / jnp.maximum(l_ref[...], 1e-30)

    wlen = WINDOW + tq
    w0 = pl.multiple_of(jnp.maximum(q0 - WINDOW, 0), tq)
    wkv = win_ref[pl.ds(w0, wlen), :]
    dist_w = t_row - (w0 + lax.broadcasted_iota(jnp.int32, (1, wlen), 1))
    s_w = _dot_nt(qs, wkv[:, 0:hd].astype(BF16)) * scale - slope * dist_w.astype(F32)
    p_w = _softmax_rows(s_w, (dist_w >= 0) & (dist_w < WINDOW))
    o_w = jnp.dot(p_w.astype(BF16), wkv[:, hd:2 * hd].astype(BF16), preferred_element_type=F32)

    gt = jax.nn.sigmoid(small_ref[:, S_NGATE:S_NGATE + 3 * nh])
    outs = []
    for h in range(nh):
        sl = slice(h * tq, (h + 1) * tq)
        outs.append(gt[:, 3 * h:3 * h + 1] * o_c[sl] + gt[:, 3 * h + 1:3 * h + 2] * o_s[sl]
                    + gt[:, 3 * h + 2:3 * h + 3] * o_w[sl])
    o_ref[...] = _rms(jnp.concatenate(outs, axis=1), g_ref[...])


def _nsa_prompt(proj, slopes, pool_w, g, bsz, n, tq=128, tk=256):
    hd, nh, gw = HEAD_DIM, N_HEADS, GROUP_WIDTH
    assert n % tk == 0 and n >= WINDOW + tq and n % SEL_BLOCK == 0
    nq = n // tq
    n_cmp, n_sel = n // CMP_BLOCK, n // SEL_BLOCK
    pw = jnp.concatenate([jnp.broadcast_to(pool_w[0][:, None], (CMP_BLOCK, hd)),
                          jnp.broadcast_to(pool_w[1][:, None], (CMP_BLOCK, hd))], axis=1)
    pair = (np.arange(n_cmp)[:, None] // 2 == np.arange(n_sel)[None, :]).astype(np.float32)
    expand = (np.arange(n_sel)[:, None] == np.arange(n)[None, :] // SEL_BLOCK)
    expand = jnp.asarray(expand.reshape(n_sel, n // tk, tk).transpose(1, 0, 2), BF16)
    return pl.pallas_call(
        functools.partial(_nsa_body, tq=tq, tk=tk, n=n),
        grid=(bsz, nq),
        in_specs=[pl.BlockSpec(memory_space=pltpu.SMEM),
                  pl.BlockSpec((tq, gw), lambda b, i: (b * nq + i, C_NQ // gw)),
                  pl.BlockSpec((n, gw), lambda b, i: (b, C_NROWS // gw)),
                  pl.BlockSpec((n, 2 * hd), lambda b, i: (b, C_WIN // (2 * hd))),
                  pl.BlockSpec((tq, 128), lambda b, i: (b * nq + i, C_SMALL // 128)),
                  _resident((CMP_BLOCK, 2 * hd)), _resident((n_cmp, n_sel)), _resident((n // tk, n_sel, tk)),
                  _resident((1, gw))],
        out_specs=pl.BlockSpec((tq, gw), lambda b, i: (b * nq + i, 0)),
        out_shape=jax.ShapeDtypeStruct((bsz * n, gw), F32),
        scratch_shapes=[pltpu.VMEM((n_cmp, 2 * hd), F32), pltpu.VMEM((nh * tq, 1), F32),
                        pltpu.VMEM((nh * tq, 1), F32), pltpu.VMEM((nh * tq, hd), F32)],
        compiler_params=_cparams(2),
        name="nsa_prompt",
    )(slopes, proj, proj, proj, proj, pw, jnp.asarray(pair), expand, g)


def _s5_body(u_ref, bre_ref, bim_ref, are_ref, aim_ref, s0r_ref, s0i_ref, cre_ref, cim_ref, d_ref, wg_ref, g_ref,
             o_ref, fr_ref, fi_ref, xr_ref, xi_ref, sr_ref, si_ref, *, bsz, tc):
    @pl.when(pl.program_id(0) == 0)
    def _():
        sr_ref[...] = s0r_ref[...]
        si_ref[...] = s0i_ref[...]

    u = u_ref[...]
    xr_ref[...] = _dotw(u, bre_ref[...])
    xi_ref[...] = _dotw(u, bim_ref[...])
    ar = jnp.broadcast_to(are_ref[...], sr_ref.shape)
    ai = jnp.broadcast_to(aim_ref[...], sr_ref.shape)

    def step(t, carry):
        sr, si = carry
        r0 = pl.multiple_of(t * bsz, bsz)
        nr = ar * sr - ai * si + xr_ref[pl.ds(r0, bsz), :]
        ni = ar * si + ai * sr + xi_ref[pl.ds(r0, bsz), :]
        xr_ref[pl.ds(r0, bsz), :] = nr
        xi_ref[pl.ds(r0, bsz), :] = ni
        return nr, ni

    sr, si = lax.fori_loop(0, tc, step, (sr_ref[...], si_ref[...]))
    sr_ref[...] = sr
    si_ref[...] = si
    fr_ref[...] = sr
    fi_ref[...] = si
    y = _dotw(xr_ref[...], cre_ref[...]) - _dotw(xi_ref[...], cim_ref[...])
    y = jax.nn.gelu(y + d_ref[...] * u)
    out = y * jax.nn.sigmoid(_dotw(y, wg_ref[...]))
    o_ref[...] = _rms(out, g_ref[...])


def _s5(u_tm, prm, s0r, s0i, g, bsz, n, tc):
    gw = GROUP_WIDTH
    ns = S5_GROUPS * S5_STATE
    rows = tc * bsz
    st = jax.ShapeDtypeStruct((bsz, ns), F32)
    return pl.pallas_call(
        functools.partial(_s5_body, bsz=bsz, tc=tc),
        grid=(n // tc,),
        in_specs=[pl.BlockSpec((rows, gw), lambda i: (i, 0)),
                  _resident((gw, ns)), _resident((gw, ns)), _resident((1, ns)), _resident((1, ns)),
                  _resident((bsz, ns)), _resident((bsz, ns)), _resident((ns, gw)), _resident((ns, gw)),
                  _resident((1, gw)), _resident((gw, gw)), _resident((1, gw))],
        out_specs=[pl.BlockSpec((rows, gw), lambda i: (i, 0)), pl.BlockSpec((bsz, ns), lambda i: (0, 0)),
                   pl.BlockSpec((bsz, ns), lambda i: (0, 0))],
        out_shape=[jax.ShapeDtypeStruct((n * bsz, gw), F32), st, st],
        scratch_shapes=[pltpu.VMEM((rows, ns), F32), pltpu.VMEM((rows, ns), F32),
                        pltpu.VMEM((bsz, ns), F32), pltpu.VMEM((bsz, ns), F32)],
        compiler_params=_cparams(1),
        name="s5",
    )(u_tm, prm["bre"], prm["bim"], prm["are"], prm["aim"], s0r, s0i, prm["cre"], prm["cim"], prm["d"], prm["wg"], g)


def _s5_params(a_re, a_im, b_re, b_im, c_re, c_im, d, log_dt, w_glu, wdt):
    dt = jnp.exp(log_dt)[:, None]
    mag = jnp.exp(dt * a_re)
    abar_re, abar_im = mag * jnp.cos(dt * a_im), mag * jnp.sin(dt * a_im)
    den = a_re * a_re + a_im * a_im
    num_re, num_im = abar_re - 1.0, abar_im
    zoh_re = (num_re * a_re + num_im * a_im) / den
    zoh_im = (num_im * a_re - num_re * a_im) / den
    bbar_re = zoh_re[..., None] * b_re - zoh_im[..., None] * b_im
    bbar_im = zoh_re[..., None] * b_im + zoh_im[..., None] * b_re
    eye = jnp.eye(S5_GROUPS, dtype=F32)
    ns = S5_GROUPS * S5_STATE
    to_in = lambda w: (w[:, :, None, :] * eye[:, None, :, None]).transpose(0, 3, 2, 1).reshape(GROUP_WIDTH, ns).astype(wdt)
    to_out = lambda w: (w[:, :, None, :] * eye[:, None, :, None]).transpose(0, 3, 2, 1).reshape(ns, GROUP_WIDTH).astype(wdt)
    return {"bre": to_in(bbar_re), "bim": to_in(bbar_im), "are": abar_re.reshape(1, ns), "aim": abar_im.reshape(1, ns),
            "cre": to_out(c_re), "cim": to_out(c_im), "d": d.reshape(1, GROUP_WIDTH), "wg": w_glu.astype(wdt)}


def _mm_bf16(a, b):
    return jnp.dot(a.astype(BF16), b.astype(BF16), preferred_element_type=F32)


def _l2n(x):
    return x * lax.rsqrt(jnp.sum(x * x, axis=-1, keepdims=True) + RMS_EPS)


def _gdn_body(qkv_ref, z_ref, small_ref, at_ref, cb_ref, s0_ref, cw_ref, alr_ref, dtr_ref, alc_ref, dtc_ref, ng_ref,
              o_ref, sfin_ref, ext_ref, s_ref, *, tc):
    j = pl.program_id(1)
    hd, nh, gw, ch = HEAD_DIM, N_HEADS, GROUP_WIDTH, GDN_CHUNK
    pad = 8

    @pl.when(j == 0)
    def _():
        ext_ref[0:pad, :] = cb_ref[0]
        s_ref[...] = s0_ref[0]

    @pl.when(j > 0)
    def _():
        ext_ref[0:pad, :] = ext_ref[tc:tc + pad, :]

    ext_ref[pad:tc + pad, :] = qkv_ref[...]
    base = pad - (CONV_W - 1)
    conv = ext_ref[base:base + tc, :] * cw_ref[0:1, :]
    for t in range(1, CONV_W):
        conv = conv + ext_ref[base + t:base + t + tc, :] * cw_ref[t:t + 1, :]
    conv = conv * jax.nn.sigmoid(conv)

    small = small_ref[...]
    g_col = -jnp.exp(alr_ref[...]) * jax.nn.softplus(small[:, S_GA:S_GA + nh] + dtr_ref[...])
    beta = jax.nn.sigmoid(small[:, S_GB:S_GB + nh])
    g_row = -jnp.exp(alc_ref[...]) * jax.nn.softplus(at_ref[0][0:nh, :] + dtc_ref[...])
    ri = lax.broadcasted_iota(jnp.int32, (tc, tc), 0)
    ci = lax.broadcasted_iota(jnp.int32, (tc, tc), 1)
    same = (ri // ch) == (ci // ch)
    tril = same & (ci <= ri)
    strict = same & (ci < ri)
    tril_f = tril.astype(F32)
    cum_col = jnp.dot(tril_f, g_col, precision=HI, preferred_element_type=F32)
    cum_row = lax.dot_general(g_row, tril_f, (((1,), (1,)), ((), ())), precision=HI,
                              preferred_element_type=F32)
    z = z_ref[...]
    outs = []
    for h in range(nh):
        gc = cum_col[:, h:h + 1]
        decay = jnp.where(tril, jnp.exp(jnp.where(tril, gc - cum_row[h:h + 1, :], 0.0)), 0.0)
        q = _l2n(conv[:, h * hd:(h + 1) * hd]) * hd ** -0.5
        k = _l2n(conv[:, gw + h * hd:gw + (h + 1) * hd])
        v = conv[:, 2 * gw + h * hd:2 * gw + (h + 1) * hd]
        bc = beta[:, h:h + 1]
        kb = k * bc
        kbb, kbf = kb.astype(BF16), k.astype(BF16)
        m = jnp.where(strict, _dot_nt(kbb, kbf) * decay, 0.0)
        nmat = -m
        qm = m
        for _ in range(int(math.log2(ch)) - 1):
            qm = _mm_bf16(qm, qm)
            nmat = nmat + qm + _mm_bf16(nmat, qm)
        eg = jnp.exp(gc)
        rhs = jnp.concatenate([v * bc, kb * eg], axis=1)
        uw = rhs + _mm_bf16(nmat, rhs)
        u, w = uw[:, 0:hd], uw[:, hd:2 * hd]
        attn = (_dot_nt(q.astype(BF16), kbf) * decay).astype(BF16)
        qg = (q * eg).astype(BF16)
        s = s_ref[h]
        o_chunks = []
        for c in range(tc // ch):
            r = slice(c * ch, (c + 1) * ch)
            sb = s.astype(BF16)
            v_new = u[r] - jnp.dot(w[r].astype(BF16), sb, preferred_element_type=F32)
            vb = v_new.astype(BF16)
            o_chunks.append(jnp.dot(qg[r], sb, preferred_element_type=F32)
                            + jnp.dot(attn[r, r], vb, preferred_element_type=F32))
            g_last = gc[(c + 1) * ch - 1:(c + 1) * ch, :]
            kd = (k[r] * jnp.exp(g_last - gc[r])).astype(BF16)
            s = s * jnp.exp(g_last) + _dot_tn(kd, vb)
        s_ref[h] = s
        o = jnp.concatenate(o_chunks, axis=0)
        o = o * lax.rsqrt(jnp.mean(o * o, axis=-1, keepdims=True) + RMS_EPS) * ng_ref[:, h * hd:(h + 1) * hd]
        zh = z[:, h * hd:(h + 1) * hd]
        outs.append(o * (zh * jax.nn.sigmoid(zh)))
    o_ref[...] = jnp.concatenate(outs, axis=1)
    sfin_ref[0] = s_ref[...]


def _gdn_prompt(proj, a_t, conv_buf8, s0, conv_w, a_log, dt_bias, norm_g, bsz, n, tc=256):
    hd, nh, gw = HEAD_DIM, N_HEADS, GROUP_WIDTH
    nb = n // tc
    return pl.pallas_call(
        functools.partial(_gdn_body, tc=tc),
        grid=(bsz, nb),
        in_specs=[pl.BlockSpec((tc, 3 * gw), lambda b, j: (b * nb + j, C_GQKV // (3 * gw))),
                  pl.BlockSpec((tc, gw), lambda b, j: (b * nb + j, C_GZ // gw)),
                  pl.BlockSpec((tc, 128), lambda b, j: (b * nb + j, C_SMALL // 128)),
                  pl.BlockSpec((1, 8, tc), lambda b, j: (b, 0, j)),
                  pl.BlockSpec((1, 8, 3 * gw), lambda b, j: (b, 0, 0)),
                  pl.BlockSpec((1, nh, hd, hd), lambda b, j: (b, 0, 0, 0)),
                  _resident((CONV_W, 3 * gw)), _resident((1, nh)), _resident((1, nh)), _resident((nh, 1)),
                  _resident((nh, 1)), _resident((1, gw))],
        out_specs=[pl.BlockSpec((tc, gw), lambda b, j: (b * nb + j, 0)),
                   pl.BlockSpec((1, nh, hd, hd), lambda b, j: (b, 0, 0, 0))],
        out_shape=[jax.ShapeDtypeStruct((bsz * n, gw), F32), jax.ShapeDtypeStruct((bsz, nh, hd, hd), F32)],
        scratch_shapes=[pltpu.VMEM((tc + 8, 3 * gw), F32), pltpu.VMEM((nh, hd, hd), F32)],
        compiler_params=_cparams(2),
        name="gdn_prompt",
    )(proj, proj, proj, a_t, conv_buf8, s0, conv_w, a_log.reshape(1, nh), dt_bias.reshape(1, nh),
      a_log.reshape(nh, 1), dt_bias.reshape(nh, 1), norm_g)


def _head_ones():
    h = np.arange(GROUP_WIDTH) // HEAD_DIM
    return jnp.asarray((h[:, None] == h[None, :]).astype(np.float32))


def _head_scores(k4, q, ones):
    return jnp.dot(k4 * q, ones, precision=HI, preferred_element_type=F32) * HEAD_DIM ** -0.5


def _tile4(x):
    return jnp.concatenate([x] * N_HEADS, axis=1)


def _fox_dec_body(pt_ref, q_ref, kvn_ref, small_ref, bf_ref, eff_ref, ones_ref, g_ref, *rest, npg, nstep):
    kv_refs, lf_refs = rest[0:npg], rest[npg:2 * npg]
    o_ref, lfo_ref, m_ref, l_ref, acc_ref, car_ref = rest[2 * npg:]
    j = pl.program_id(1)
    gw, hd, nh = GROUP_WIDTH, HEAD_DIM, N_HEADS
    q = q_ref[0]
    ones = ones_ref[...]

    @pl.when(j == 0)
    def _():
        lf_small = jax.nn.log_sigmoid(small_ref[0] + bf_ref[...])
        lfo_ref[0] = lf_small
        kvn = kvn_ref[0]
        m_ref[...] = _head_scores(kvn[:, 0:gw], q, ones)
        l_ref[...] = jnp.ones(l_ref.shape, F32)
        acc_ref[...] = kvn[:, gw:2 * gw]
        car_ref[...] = jnp.dot(lf_small, eff_ref[...], precision=HI, preferred_element_type=F32)

    ri = lax.broadcasted_iota(jnp.int32, (128, 128), 0)
    ci = lax.broadcasted_iota(jnp.int32, (128, 128), 1)
    upper = (ci > ri).astype(F32)
    m, l, acc, car = m_ref[...], l_ref[...], acc_ref[...], car_ref[...]
    for i in range(npg):
        kv = kv_refs[i][0]
        lf = lf_refs[i][0]
        lfe = jnp.concatenate([jnp.broadcast_to(lf[:, h:h + 1], (lf.shape[0], hd)) for h in range(nh)], axis=1)
        suf = jnp.dot(upper, lfe, precision=HI, preferred_element_type=F32)
        s = _head_scores(kv[:, 0:gw], q, ones) + car + suf
        m_new = jnp.maximum(m, jnp.max(s, axis=0, keepdims=True))
        a = jnp.exp(m - m_new)
        p = jnp.exp(s - m_new)
        l = a * l + jnp.sum(p, axis=0, keepdims=True)
        acc = a * acc + jnp.sum(p * kv[:, gw:2 * gw], axis=0, keepdims=True)
        m = m_new
        car = car + jnp.sum(lfe, axis=0, keepdims=True)
    m_ref[...], l_ref[...], acc_ref[...], car_ref[...] = m, l, acc, car

    @pl.when(j == nstep - 1)
    def _():
        o_ref[0] = _rms(acc / l, g_ref[...])


def _fox_decode(proj3, page_table, kv_cache, lf_cache, bf_pad, g, npg=8):
    bsz = proj3.shape[0]
    n_pages = page_table.shape[1]
    gw, nh = GROUP_WIDTH, N_HEADS
    nstep = n_pages // npg
    eff = np.zeros((128, gw), np.float32)
    for h in range(nh):
        eff[S_FF + h, h * HEAD_DIM:(h + 1) * HEAD_DIM] = 1.0
    page = lambda i: (lambda b, j, pt: (pt[b, n_pages - 1 - (j * npg + i)], 0, 0))
    const2 = lambda b, j, pt: (0, 0)
    grid_spec = pltpu.PrefetchScalarGridSpec(
        num_scalar_prefetch=1,
        grid=(bsz, nstep),
        in_specs=[pl.BlockSpec((1, 1, gw), lambda b, j, pt: (b, 0, C_FQ // gw)),
                  pl.BlockSpec((1, 1, 2 * gw), lambda b, j, pt: (b, 0, C_FKV // (2 * gw))),
                  pl.BlockSpec((1, 1, 128), lambda b, j, pt: (b, 0, C_SMALL // 128)),
                  pl.BlockSpec((1, 128), const2), pl.BlockSpec((128, gw), const2), pl.BlockSpec((gw, gw), const2),
                  pl.BlockSpec((1, gw), const2)]
        + [pl.BlockSpec((1, 128, 2 * gw), page(i)) for i in range(npg)]
        + [pl.BlockSpec((1, 128, nh), page(i)) for i in range(npg)],
        out_specs=[pl.BlockSpec((1, 1, gw), lambda b, j, pt: (b, 0, 0)),
                   pl.BlockSpec((1, 1, 128), lambda b, j, pt: (b, 0, 0))],
        scratch_shapes=[pltpu.VMEM((1, gw), F32)] * 4,
    )
    return pl.pallas_call(
        functools.partial(_fox_dec_body, npg=npg, nstep=nstep),
        grid_spec=grid_spec,
        out_shape=[jax.ShapeDtypeStruct((bsz, 1, gw), F32), jax.ShapeDtypeStruct((bsz, 1, 128), F32)],
        compiler_params=_cparams(2),
        name="fox_decode",
    )(page_table, proj3, proj3, proj3, bf_pad, jnp.asarray(eff), _head_ones(), g,
      *([kv_cache] * npg), *([lf_cache] * npg))


def _nsa_dec_cmp_body(pt_ref, q_ref, pw_ref, slope_ref, ones_ref, *rest, npg, nstep, p0):
    pg_refs = rest[0:npg]
    oc_ref, idx_ref, kvc_ref, imp_ref = rest[npg:]
    j = pl.program_id(1)
    gw, hd, nh = GROUP_WIDTH, HEAD_DIM, N_HEADS
    per_page = 128 // CMP_BLOCK
    pw = pw_ref[...]
    pooled = [jnp.sum((pg_refs[i][0] * pw).reshape(per_page, CMP_BLOCK, 2 * hd), axis=1) for i in range(npg)]
    r0 = pl.multiple_of(j * (npg * per_page), npg * per_page)
    kvc_ref[pl.ds(r0, npg * per_page), :] = jnp.concatenate(pooled, axis=0)

    @pl.when(j == nstep - 1)
    def _():
        n_cmp = kvc_ref.shape[0]
        n_sel = n_cmp // 2
        q = q_ref[0]
        kvc = kvc_ref[...]
        cmp_end = (lax.broadcasted_iota(jnp.int32, (n_cmp, 1), 0) + 1) * CMP_BLOCK - 1
        dist = p0 - cmp_end
        s = _head_scores(_tile4(kvc[:, 0:hd]), q, ones_ref[...]) - slope_ref[...] * dist.astype(F32)
        ok = dist >= 0
        s = jnp.where(ok, s, NEG)
        p = jnp.where(ok, jnp.exp(s - jnp.max(s, axis=0, keepdims=True)), 0.0)
        p = p / jnp.maximum(jnp.sum(p, axis=0, keepdims=True), 1e-30)
        oc_ref[0] = jnp.sum(p * _tile4(kvc[:, hd:2 * hd]), axis=0, keepdims=True)
        imp = p[:, 0:1]
        for h in range(1, nh):
            imp = imp + p[:, h * hd:h * hd + 1]
        imp_ref[...] = jnp.broadcast_to(imp, imp_ref.shape)
        colm = imp_ref[pl.ds(0, n_sel, stride=2), :] + imp_ref[pl.ds(1, n_sel, stride=2), :]
        ri = lax.broadcasted_iota(jnp.int32, (n_sel, n_sel), 0)
        ci = lax.broadcasted_iota(jnp.int32, (n_sel, n_sel), 1)
        colm = jnp.where(ri == 0, FORCE_SCORE, colm)
        rowm = colm.T
        beats = (rowm > colm) | ((rowm == colm) & (ci < ri))
        rank = jnp.sum(beats.astype(F32), axis=1, keepdims=True)
        sel = (rank < SEL_TOPK - 1).astype(F32)
        pos = jnp.dot((ci < ri).astype(F32), jnp.broadcast_to(sel, (n_sel, n_sel)), preferred_element_type=F32)
        onehot = jnp.where((sel > 0.5) & (pos == ci.astype(F32)), ri.astype(F32), 0.0)
        idx_ref[0] = jnp.sum(onehot, axis=0, keepdims=True).astype(jnp.int32)


def _nsa_dec_cmp(proj3, page_table, cache2, pw128, slope_e, p0, npg=16):
    bsz = proj3.shape[0]
    n_pages = page_table.shape[1]
    gw, hd = GROUP_WIDTH, HEAD_DIM
    nstep = n_pages // npg
    n_cmp = p0 // CMP_BLOCK
    assert p0 % SEL_BLOCK == 0 and n_cmp // 2 == 128 and p0 == n_pages * 128
    page = lambda i: (lambda b, j, pt: (pt[b, j * npg + i], 0, 0))
    const2 = lambda b, j, pt: (0, 0)
    grid_spec = pltpu.PrefetchScalarGridSpec(
        num_scalar_prefetch=1,
        grid=(bsz, nstep),
        in_specs=[pl.BlockSpec((1, 1, gw), lambda b, j, pt: (b, 0, C_NQ // gw)),
                  pl.BlockSpec((128, 2 * hd), const2), pl.BlockSpec((1, gw), const2), pl.BlockSpec((gw, gw), const2)]
        + [pl.BlockSpec((1, 128, 2 * hd), page(i)) for i in range(npg)],
        out_specs=[pl.BlockSpec((1, 1, gw), lambda b, j, pt: (b, 0, 0)),
                   pl.BlockSpec((1, 1, 128), lambda b, j, pt: (b, 0, 0))],
        scratch_shapes=[pltpu.VMEM((n_cmp, 2 * hd), F32), pltpu.VMEM((n_cmp, 128), F32)],
    )
    return pl.pallas_call(
        functools.partial(_nsa_dec_cmp_body, npg=npg, nstep=nstep, p0=p0),
        grid_spec=grid_spec,
        out_shape=[jax.ShapeDtypeStruct((bsz, 1, gw), F32), jax.ShapeDtypeStruct((bsz, 1, 128), jnp.int32)],
        compiler_params=_cparams(2),
        name="nsa_dec_cmp",
    )(page_table, proj3, pw128, slope_e, _head_ones(), *([cache2] * npg))


def _nsa_dec_sel_body(pt_ref, idx_ref, q_ref, new_ref, wnew_ref, small_ref, oc_ref, win_ref, slope_ref, ones_ref,
                      eg_ref, g_ref, *rest, nsel, p0):
    blk_refs = rest[0:nsel]
    o_ref = rest[nsel]
    b = pl.program_id(0)
    hd = HEAD_DIM
    q = q_ref[0]
    ones = ones_ref[...]
    slope = slope_ref[...]
    new = new_ref[0]
    s_new = _head_scores(_tile4(new[:, 2 * hd:3 * hd]), q, ones)
    v_new = _tile4(new[:, 3 * hd:4 * hd])
    ss, vs = [], []
    r = lax.broadcasted_iota(jnp.int32, (SEL_BLOCK, 1), 0)
    for i in range(nsel):
        blk = blk_refs[i][0]
        dist = (p0 - idx_ref[b, i] * SEL_BLOCK) - r
        ss.append(_head_scores(_tile4(blk[:, 0:hd]), q, ones) - slope * dist.astype(F32))
        vs.append(_tile4(blk[:, hd:2 * hd]))
    s = jnp.concatenate(ss, axis=0)
    v = jnp.concatenate(vs, axis=0)
    m = jnp.maximum(jnp.max(s, axis=0, keepdims=True), s_new)
    p, p_new = jnp.exp(s - m), jnp.exp(s_new - m)
    o_s = (jnp.sum(p * v, axis=0, keepdims=True) + p_new * v_new) / (jnp.sum(p, axis=0, keepdims=True) + p_new)
    win = win_ref[0]
    wn = wnew_ref[0]
    nw = win.shape[0]
    dist_w = nw - lax.broadcasted_iota(jnp.int32, (nw, 1), 0)
    ok = dist_w < WINDOW
    s_w = jnp.where(ok, _head_scores(_tile4(win[:, 0:hd]), q, ones) - slope * dist_w.astype(F32), NEG)
    sw_new = _head_scores(_tile4(wn[:, 0:hd]), q, ones)
    mw = jnp.maximum(jnp.max(s_w, axis=0, keepdims=True), sw_new)
    pw_, pw_new = jnp.where(ok, jnp.exp(s_w - mw), 0.0), jnp.exp(sw_new - mw)
    o_w = ((jnp.sum(pw_ * _tile4(win[:, hd:2 * hd]), axis=0, keepdims=True) + pw_new * _tile4(wn[:, hd:2 * hd]))
           / (jnp.sum(pw_, axis=0, keepdims=True) + pw_new))
    gt = jax.nn.sigmoid(small_ref[0])
    ge = [jnp.dot(gt, eg_ref[c], precision=HI, preferred_element_type=F32) for c in range(3)]
    o_ref[0] = _rms(ge[0] * oc_ref[0] + ge[1] * o_s + ge[2] * o_w, g_ref[...])


def _nsa_dec_sel(proj3, page_table, idx, o_c, cache_half, win_state, slope_e, g, p0):
    bsz = proj3.shape[0]
    gw, hd, nh = GROUP_WIDTH, HEAD_DIM, N_HEADS
    nsel = SEL_TOPK - 1
    nw = win_state.shape[1]
    eg = np.zeros((3, 128, gw), np.float32)
    for c in range(3):
        for h in range(nh):
            eg[c, S_NGATE + 3 * h + c, h * hd:(h + 1) * hd] = 1.0
    n_blocks = cache_half.shape[0]
    per_page = (p0 // page_table.shape[1]) // SEL_BLOCK

    def blk(i):
        def index_map(b, pt, ix):
            sel = jnp.clip(ix[b, i], 0, p0 // SEL_BLOCK - 1)
            page = jnp.clip(pt[b, sel // per_page], 0, n_blocks // per_page - 1)
            return (page * per_page + sel % per_page, 0, 1)
        return index_map
    const2 = lambda b, pt, ix: (0, 0)
    grid_spec = pltpu.PrefetchScalarGridSpec(
        num_scalar_prefetch=2,
        grid=(bsz,),
        in_specs=[pl.BlockSpec((1, 1, gw), lambda b, pt, ix: (b, 0, C_NQ // gw)),
                  pl.BlockSpec((1, 1, gw), lambda b, pt, ix: (b, 0, C_NROWS // gw)),
                  pl.BlockSpec((1, 1, 2 * hd), lambda b, pt, ix: (b, 0, C_WIN // (2 * hd))),
                  pl.BlockSpec((1, 1, 128), lambda b, pt, ix: (b, 0, C_SMALL // 128)),
                  pl.BlockSpec((1, 1, gw), lambda b, pt, ix: (b, 0, 0)),
                  pl.BlockSpec((1, nw, 2 * hd), lambda b, pt, ix: (b, 0, 0)),
                  pl.BlockSpec((1, gw), const2), pl.BlockSpec((gw, gw), const2),
                  pl.BlockSpec((3, 128, gw), lambda b, pt, ix: (0, 0, 0)), pl.BlockSpec((1, gw), const2)]
        + [pl.BlockSpec((1, SEL_BLOCK, 2 * hd), blk(i)) for i in range(nsel)],
        out_specs=pl.BlockSpec((1, 1, gw), lambda b, pt, ix: (b, 0, 0)),
    )
    return pl.pallas_call(
        functools.partial(_nsa_dec_sel_body, nsel=nsel, p0=p0),
        grid_spec=grid_spec,
        out_shape=jax.ShapeDtypeStruct((bsz, 1, gw), F32),
        compiler_params=_cparams(1),
        name="nsa_dec_sel",
    )(page_table, idx, proj3, proj3, proj3, proj3, o_c, win_state, slope_e, _head_ones(), jnp.asarray(eg), g,
      *([cache_half] * nsel))


def _gdn_dec_body(qkv_ref, z_ref, small_ref, cb_ref, s0_ref, cw_ref, al_ref, dt_ref, ng_ref, o_ref, s_ref):
    hd, nh, gw = HEAD_DIM, N_HEADS, GROUP_WIDTH
    cb = cb_ref[0]
    conv = cb[0:1] * cw_ref[0:1, :]
    for t in range(1, CONV_W - 1):
        conv = conv + cb[t:t + 1] * cw_ref[t:t + 1, :]
    conv = conv + qkv_ref[0] * cw_ref[CONV_W - 1:CONV_W, :]
    conv = conv * jax.nn.sigmoid(conv)
    small = small_ref[0]
    g = -jnp.exp(al_ref[...]) * jax.nn.softplus(small[:, S_GA:S_GA + nh] + dt_ref[...])
    beta = jax.nn.sigmoid(small[:, S_GB:S_GB + nh])
    z = z_ref[0]
    ri = lax.broadcasted_iota(jnp.int32, (hd, hd), 0)
    ci = lax.broadcasted_iota(jnp.int32, (hd, hd), 1)
    outs = []
    for h in range(nh):
        q = _l2n(conv[:, h * hd:(h + 1) * hd]) * hd ** -0.5
        k = _l2n(conv[:, gw + h * hd:gw + (h + 1) * hd])
        v = conv[:, 2 * gw + h * hd:2 * gw + (h + 1) * hd]
        eg = jnp.exp(g[:, h:h + 1])
        bc = beta[:, h:h + 1]
        s = s0_ref[0, h]
        v_new = v * bc - jnp.dot(k * bc * eg, s, precision=HI, preferred_element_type=F32)
        o = (jnp.dot(q * eg, s, precision=HI, preferred_element_type=F32)
             + jnp.sum(q * k, axis=-1, keepdims=True) * v_new)
        k_col = jnp.sum(jnp.where(ri == ci, jnp.broadcast_to(k, (hd, hd)), 0.0), axis=1, keepdims=True)
        s_ref[0, h] = s * eg + k_col * v_new
        o = o * lax.rsqrt(jnp.mean(o * o, axis=-1, keepdims=True) + RMS_EPS) * ng_ref[:, h * hd:(h + 1) * hd]
        zh = z[:, h * hd:(h + 1) * hd]
        outs.append(o * (zh * jax.nn.sigmoid(zh)))
    o_ref[0] = jnp.concatenate(outs, axis=1)


def _gdn_decode(proj3, conv_buf, s0, conv_w, a_log, dt_bias, norm_g):
    bsz = proj3.shape[0]
    hd, nh, gw = HEAD_DIM, N_HEADS, GROUP_WIDTH
    const2 = lambda b: (0, 0)
    return pl.pallas_call(
        _gdn_dec_body,
        grid=(bsz,),
        in_specs=[pl.BlockSpec((1, 1, 3 * gw), lambda b: (b, 0, C_GQKV // (3 * gw))),
                  pl.BlockSpec((1, 1, gw), lambda b: (b, 0, C_GZ // gw)),
                  pl.BlockSpec((1, 1, 128), lambda b: (b, 0, C_SMALL // 128)),
                  pl.BlockSpec((1, CONV_W - 1, 3 * gw), lambda b: (b, 0, 0)),
                  pl.BlockSpec((1, nh, hd, hd), lambda b: (b, 0, 0, 0)),
                  pl.BlockSpec((CONV_W, 3 * gw), const2), pl.BlockSpec((1, nh), const2), pl.BlockSpec((1, nh), const2),
                  pl.BlockSpec((1, gw), const2)],
        out_specs=[pl.BlockSpec((1, 1, gw), lambda b: (b, 0, 0)), pl.BlockSpec((1, nh, hd, hd), lambda b: (b, 0, 0, 0))],
        out_shape=[jax.ShapeDtypeStruct((bsz, 1, gw), F32), jax.ShapeDtypeStruct((bsz, nh, hd, hd), F32)],
        compiler_params=_cparams(1),
        name="gdn_decode",
    )(proj3, proj3, proj3, conv_buf, s0, conv_w, a_log.reshape(1, nh), dt_bias.reshape(1, nh), norm_g)


def kernel(x_prompt, x_sample, cache_nsa_kv, cache_fox_kv, cache_fox_logf, state_nsa_win, state_gdn, state_gdn_conv, state_s5_re, state_s5_im, page_table, w_in, nsa_pool, s5_a_re, s5_a_im, s5_b_re, s5_b_im, s5_c_re, s5_c_im, s5_d, s5_log_dt, s5_w_glu, gdn_conv, gdn_a_log, gdn_dt_bias, fox_b_f, mix_norm, w_out, ln1_g, ln1_b, ln2_g, ln2_b, router_w, router_b, exp_w1, exp_w3, exp_w2):
    B, L, D = x_prompt.shape
    BS = x_sample.shape[0]
    hd, nh, gw = HEAD_DIM, N_HEADS, GROUP_WIDTH
    ns = S5_GROUPS * S5_STATE
    n_pool = cache_nsa_kv.shape[1]
    p0 = page_table.shape[1] * cache_nsa_kv.shape[2]
    slopes = 2.0 ** (-8.0 * (jnp.arange(nh, dtype=F32) + 1.0) / nh)
    slope_e = jnp.repeat(slopes, hd).reshape(1, gw)
    xp = x_prompt.reshape(B * L, D)
    xs = x_sample.reshape(BS, D)
    acc_p = [[] for _ in range(8)]
    acc_s = [[] for _ in range(8)]
    for l in range(DEPTH):
        wp_f = _permute_w_in(w_in[l])
        wo_f = w_out[l]
        w1_f = exp_w1[l].transpose(1, 0, 2).reshape(D, -1)
        w3_f = exp_w3[l].transpose(1, 0, 2).reshape(D, -1)
        w2_f = exp_w2[l].reshape(-1, D)
        wp, wo, w1, w3, w2 = [w.astype(BF16) for w in (wp_f, wo_f, w1_f, w3_f, w2_f)]
        rb = router_b.reshape(1, -1)
        ln1 = (ln1_g[l].reshape(1, D), ln1_b[l].reshape(1, D))
        ln2 = (ln2_g[l].reshape(1, D), ln2_b[l].reshape(1, D))
        g_nsa, g_s5, g_gdn, g_fox = [g.reshape(1, gw) for g in jnp.split(mix_norm[l], N_MIXERS)]
        s5_args = (s5_a_re[l], s5_a_im[l], s5_b_re[l], s5_b_im[l], s5_c_re[l], s5_c_im[l], s5_d[l],
                   s5_log_dt[l], s5_w_glu[l])
        prm, prm_f = _s5_params(*s5_args, BF16), _s5_params(*s5_args, F32)

        proj = _in_proj(xp, wp, 256)
        o_nsa = _nsa_prompt(proj, slopes, nsa_pool[l], g_nsa, B, L)
        small = proj[:, C_SMALL:C_SMALL + 128].reshape(B, L, 128)
        logf_t, cum_t = _fox_prep(small[:, :, S_FF:S_FF + nh].transpose(0, 2, 1), fox_b_f[l].reshape(nh, 1))
        o_fox = _fox_prompt(proj, cum_t.transpose(0, 2, 1), cum_t, g_fox, B, L, 256)
        u_tm = proj[:, C_S5U:C_S5U + gw].reshape(B, L, gw).transpose(1, 0, 2).reshape(L * B, gw)
        zst = jnp.zeros((B, ns), F32)
        o_s5, s5r, s5i = _s5(u_tm, prm, zst, zst, g_s5, B, L, 256)
        o_s5 = o_s5.reshape(L, B, gw).transpose(1, 0, 2).reshape(B * L, gw)
        a_t = jnp.pad(small[:, :, S_GA:S_GA + nh].transpose(0, 2, 1), ((0, 0), (0, 8 - nh), (0, 0)))
        o_gdn, gdn_st = _gdn_prompt(proj, a_t, jnp.zeros((B, 8, 3 * gw), F32), jnp.zeros((B, nh, hd, hd), F32),
                                    gdn_conv[l], gdn_a_log[l], gdn_dt_bias[l], g_gdn, B, L)
        xp = _out_proj_ln(xp, [o_nsa, o_s5, o_gdn, o_fox], wo, *ln1, 256)
        xp = _moe_ln(xp, router_w, rb, w1, w3, w2, *ln2, 256)
        st = (proj[:, C_NROWS:C_NROWS + gw].reshape(B, L, 4, hd),
              proj[:, C_FKV:C_FKV + 2 * gw].reshape(B, L, 2, nh, hd),
              logf_t.transpose(0, 2, 1),
              proj[:, C_WIN:C_WIN + 2 * hd].reshape(B, L, 2, hd)[:, L - min(WINDOW, L):],
              gdn_st,
              proj[:, C_GQKV:C_GQKV + 3 * gw].reshape(B, L, 3 * gw)[:, L - (CONV_W - 1):],
              s5r.reshape(B, S5_GROUPS, S5_STATE), s5i.reshape(B, S5_GROUPS, S5_STATE))
        for a, v in zip(acc_p, st):
            a.append(v)

        proj_s = _in_proj(xs, wp_f, BS)
        proj3 = proj_s.reshape(BS, 1, PROJ_COLS)
        pw = jnp.concatenate([jnp.broadcast_to(nsa_pool[l][0][:, None], (CMP_BLOCK, hd)),
                              jnp.broadcast_to(nsa_pool[l][1][:, None], (CMP_BLOCK, hd))], axis=1)
        o_c, idx = _nsa_dec_cmp(proj3, page_table, cache_nsa_kv[l].reshape(n_pool, -1, 4 * hd),
                                jnp.tile(pw, (128 // CMP_BLOCK, 1)), slope_e, p0)
        win_state = state_nsa_win[l]
        bf_pad = jnp.zeros((1, 128), F32).at[0, S_FF:S_FF + nh].set(fox_b_f[l])
        o_nsa = _nsa_dec_sel(proj3, page_table, idx.reshape(BS, 128), o_c,
                             cache_nsa_kv[l].reshape(-1, SEL_BLOCK, 4 * hd),
                             win_state.reshape(BS, -1, 2 * hd), slope_e, g_nsa, p0)
        o_fox, lfo = _fox_decode(proj3, page_table, cache_fox_kv[l].reshape(n_pool, -1, 2 * gw),
                                 cache_fox_logf[l], bf_pad, g_fox)
        o_s5, s5r, s5i = _s5(proj_s[:, C_S5U:C_S5U + gw], prm_f, state_s5_re[l].reshape(BS, ns),
                             state_s5_im[l].reshape(BS, ns), g_s5, BS, 1, 1)
        o_gdn, gdn_st = _gdn_decode(proj3, state_gdn_conv[l], state_gdn[l], gdn_conv[l], gdn_a_log[l],
                                    gdn_dt_bias[l], g_gdn)
        xs = _out_proj_ln(xs, [o_nsa.reshape(BS, gw), o_s5, o_gdn.reshape(BS, gw), o_fox.reshape(BS, gw)], wo_f, *ln1, BS)
        xs = _moe_dec(xs, router_w, rb, w1_f, w3_f, w2_f, *ln2)
        st = (proj_s[:, C_NROWS:C_NROWS + gw].reshape(BS, 1, 4, hd),
              proj_s[:, C_FKV:C_FKV + 2 * gw].reshape(BS, 1, 2, nh, hd),
              lfo[:, :, S_FF:S_FF + nh],
              jnp.concatenate([win_state[:, 1:], proj_s[:, C_WIN:C_WIN + 2 * hd].reshape(BS, 1, 2, hd)], axis=1),
              gdn_st,
              jnp.concatenate([state_gdn_conv[l][:, 1:], proj_s[:, C_GQKV:C_GQKV + 3 * gw].reshape(BS, 1, 3 * gw)], axis=1),
              s5r.reshape(BS, S5_GROUPS, S5_STATE), s5i.reshape(BS, S5_GROUPS, S5_STATE))
        for a, v in zip(acc_s, st):
            a.append(v)
    nsa_rows_p, fox_kv_p, fox_logf_p, nsa_win_p, gdn_p, gdn_conv_p, s5_re_p, s5_im_p = [jnp.stack(a) for a in acc_p]
    nsa_rows_s, fox_kv_s, fox_logf_s, nsa_win_s, gdn_s, gdn_conv_s, s5_re_s, s5_im_s = [jnp.stack(a) for a in acc_s]
    return (xp.reshape(B, L, D), xs.reshape(BS, 1, D), nsa_rows_p, nsa_rows_s, fox_kv_p, fox_kv_s, fox_logf_p, fox_logf_s,
            nsa_win_p, nsa_win_s, gdn_p, gdn_s, gdn_conv_p, gdn_conv_s, s5_re_p, s5_re_s, s5_im_p, s5_im_s)
```

```python
import functools
import math

import numpy as np
import jax
import jax.numpy as jnp
from jax import lax
from jax.experimental import pallas as pl
from jax.experimental.pallas import tpu as pltpu

F32 = jnp.float32
BF16 = jnp.bfloat16
HI = lax.Precision.HIGHEST

DEPTH = 4
N_MIXERS = 4
GROUP_WIDTH = 256
HEAD_DIM = 64
N_HEADS = 4
CMP_BLOCK = 32
SEL_BLOCK = 64
SEL_TOPK = 16
WINDOW = 512
FORCE_SCORE = 1.0e4
S5_GROUP = 16
S5_GROUPS = 16
S5_STATE = 64
CONV_W = 4
GDN_CHUNK = 64
N_EXPERTS = 16
N_EXPERT_GROUPS = 4
EXPERTS_PER_GROUP = 4
D_EXPERT = 256
ALPHA = (2.0 * DEPTH) ** 0.25
LN_EPS = 1e-5
RMS_EPS = 1e-6
NEG = -1e30

VMEM_LIMIT = 56 * 1024 * 1024

C_GQKV, C_NQ, C_FKV, C_NROWS, C_S5U, C_GZ, C_FQ, C_WIN, C_SMALL = 0, 768, 1024, 1536, 1792, 2048, 2304, 2560, 2688
PROJ_COLS = 2816
S_NGATE, S_GA, S_GB, S_FF = 0, 12, 16, 20


_IN_PROJ_SEGS = ((908, 1676), (0, 256), (2196, 2708), (256, 512), (652, 908), (1684, 1940), (1940, 2196),
                 (512, 640), (640, 652), (1676, 1680), (1680, 1684), (2708, 2712))


def _in_proj_perm():
    return np.concatenate([np.arange(a, b) for a, b in _IN_PROJ_SEGS])


def _permute_w_in(w):
    used = sum(b - a for a, b in _IN_PROJ_SEGS)
    parts = [w[:, a:b] for a, b in _IN_PROJ_SEGS] + [jnp.zeros((w.shape[0], PROJ_COLS - used), w.dtype)]
    return jnp.concatenate(parts, axis=1)


def _dotw(a, w):
    if w.dtype == BF16:
        return jnp.dot(a.astype(BF16), w, preferred_element_type=F32)
    return jnp.dot(a, w, precision=HI, preferred_element_type=F32)


def _cparams(n_axes):
    return pltpu.CompilerParams(dimension_semantics=("arbitrary",) * n_axes, vmem_limit_bytes=VMEM_LIMIT)


def _resident(shape):
    nd = len(shape)
    return pl.BlockSpec(shape, lambda *_: (0,) * nd, pipeline_mode=pl.Buffered(1))


def _dot_nt(a, b):
    return lax.dot_general(a, b, (((1,), (1,)), ((), ())), preferred_element_type=F32)


def _dot_tn(a, b):
    return lax.dot_general(a, b, (((0,), (0,)), ((), ())), preferred_element_type=F32)


def _rms(x, g):
    return x * lax.rsqrt(jnp.mean(x * x, axis=-1, keepdims=True) + RMS_EPS) * g


def _ln(x, g, b):
    mu = jnp.mean(x, axis=-1, keepdims=True)
    xc = x - mu
    var = jnp.mean(xc * xc, axis=-1, keepdims=True)
    return xc * lax.rsqrt(var + LN_EPS) * g + b


def _in_proj_body(x_ref, w_ref, o_ref, *, nch):
    x = x_ref[...]
    x = x.astype(BF16) if w_ref.dtype == BF16 else x
    for j in range(0, o_ref.shape[1], nch):
        o_ref[:, j:j + nch] = _dotw(x, w_ref[:, j:j + nch])


def _in_proj(x2d, w, tm):
    m, k = x2d.shape
    n = w.shape[1]
    return pl.pallas_call(
        functools.partial(_in_proj_body, nch=256),
        grid=(m // tm,),
        in_specs=[pl.BlockSpec((tm, k), lambda i: (i, 0)), _resident((k, n))],
        out_specs=pl.BlockSpec((tm, n), lambda i: (i, 0)),
        out_shape=jax.ShapeDtypeStruct((m, n), F32),
        compiler_params=_cparams(1),
        name="in_proj",
    )(x2d, w)


def _out_proj_body(x_ref, o0_ref, o1_ref, o2_ref, o3_ref, w_ref, g_ref, b_ref, out_ref):
    gw = GROUP_WIDTH
    y = _dotw(o0_ref[...], w_ref[0:gw, :])
    y = y + _dotw(o1_ref[...], w_ref[gw:2 * gw, :])
    y = y + _dotw(o2_ref[...], w_ref[2 * gw:3 * gw, :])
    y = y + _dotw(o3_ref[...], w_ref[3 * gw:4 * gw, :])
    out_ref[...] = _ln(ALPHA * x_ref[...] + y, g_ref[...], b_ref[...])


def _out_proj_ln(x2d, mixers, w_out, g, b, tm):
    m, d = x2d.shape
    gw = GROUP_WIDTH
    row = lambda i: (i, 0)
    return pl.pallas_call(
        _out_proj_body,
        grid=(m // tm,),
        in_specs=[pl.BlockSpec((tm, d), row)] + [pl.BlockSpec((tm, gw), row)] * 4
        + [_resident((d, d)), _resident((1, d)), _resident((1, d))],
        out_specs=pl.BlockSpec((tm, d), row),
        out_shape=jax.ShapeDtypeStruct((m, d), F32),
        compiler_params=_cparams(1),
        name="out_proj_ln",
    )(x2d, *mixers, w_out, g, b)


def _moe_gate_t(lt):
    m = jnp.max(lt, axis=0, keepdims=True)
    p = jnp.exp(lt - m)
    probs = p / jnp.sum(p, axis=0, keepdims=True)
    rows = [probs[e:e + 1, :] for e in range(N_EXPERTS)]
    n = EXPERTS_PER_GROUP
    scores = []
    for g in range(N_EXPERT_GROUPS):
        r = rows[g * n:(g + 1) * n]
        best = None
        for i in range(n):
            for j in range(i + 1, n):
                s = r[i] + r[j]
                best = s if best is None else jnp.maximum(best, s)
        scores.append(best)
    grp = jnp.zeros_like(scores[0], dtype=jnp.int32)
    top = scores[0]
    for g in range(1, N_EXPERT_GROUPS):
        take = scores[g] > top
        grp = jnp.where(take, g, grp)
        top = jnp.where(take, scores[g], top)
    vals = []
    for j in range(n):
        v = rows[j]
        for g in range(1, N_EXPERT_GROUPS):
            v = jnp.where(grp == g, rows[g * n + j], v)
        vals.append(v)

    def first_argmax(vs):
        idx = jnp.zeros_like(grp)
        best = vs[0]
        for j in range(1, n):
            take = vs[j] > best
            idx = jnp.where(take, j, idx)
            best = jnp.where(take, vs[j], best)
        return best, idx

    v1, i1 = first_argmax(vals)
    v2, i2 = first_argmax([jnp.where(i1 == j, -jnp.inf, vals[j]) for j in range(n)])
    tot = v1 + v2
    w1, w2 = v1 / tot, v2 / tot
    e1, e2 = grp * n + i1, grp * n + i2
    gate = [jnp.where(e1 == e, w1, 0.0) + jnp.where(e2 == e, w2, 0.0) for e in range(N_EXPERTS)]
    return jnp.concatenate(gate, axis=0)


def _moe_body(x_ref, rw_ref, rb_ref, w1_ref, w3_ref, w2_ref, g_ref, b_ref, out_ref, *, epc):
    x = x_ref[...]
    tm = x.shape[0]
    logits = jnp.dot(x, rw_ref[...], precision=HI, preferred_element_type=F32) + rb_ref[...]
    gate = _moe_gate_t(logits.T).T
    xb = x.astype(BF16)
    cw = epc * D_EXPERT
    y = jnp.zeros((tm, x.shape[1]), F32)
    for c in range(N_EXPERTS // epc):
        a = jnp.dot(xb, w1_ref[:, c * cw:(c + 1) * cw], preferred_element_type=F32)
        b = jnp.dot(xb, w3_ref[:, c * cw:(c + 1) * cw], preferred_element_type=F32)
        ge = jnp.concatenate(
            [jnp.broadcast_to(gate[:, e:e + 1], (tm, D_EXPERT)) for e in range(c * epc, (c + 1) * epc)], axis=1)
        h = (a * jax.nn.sigmoid(a)) * b * ge
        y = y + jnp.dot(h.astype(BF16), w2_ref[c * cw:(c + 1) * cw, :], preferred_element_type=F32)
    out_ref[...] = _ln(ALPHA * x + y, g_ref[...], b_ref[...])


def _moe_ln(x2d, rw, rb, w1, w3, w2, g, b, tm):
    m, d = x2d.shape
    ne = N_EXPERTS * D_EXPERT
    row = lambda i: (i, 0)
    return pl.pallas_call(
        functools.partial(_moe_body, epc=4),
        grid=(m // tm,),
        in_specs=[pl.BlockSpec((tm, d), row), _resident((d, N_EXPERTS)), _resident((1, N_EXPERTS)),
                  _resident((d, ne)), _resident((d, ne)), _resident((ne, d)), _resident((1, d)), _resident((1, d))],
        out_specs=pl.BlockSpec((tm, d), row),
        out_shape=jax.ShapeDtypeStruct((m, d), F32),
        compiler_params=_cparams(1),
        name="moe_ln",
    )(x2d, rw, rb, w1, w3, w2, g, b)


def _moe_dec_body(x_ref, rw_ref, rb_ref, w1_ref, w3_ref, w2_ref, g_ref, b_ref, out_ref, ge_ref, acc_ref, *, epc):
    c = pl.program_id(0)
    x = x_ref[...]
    tm = x.shape[0]

    @pl.when(c == 0)
    def _():
        logits = jnp.dot(x, rw_ref[...], precision=HI, preferred_element_type=F32) + rb_ref[...]
        gate = _moe_gate_t(logits.T).T
        for cc in range(N_EXPERTS // epc):
            ge_ref[cc] = jnp.concatenate(
                [jnp.broadcast_to(gate[:, e:e + 1], (tm, D_EXPERT)) for e in range(cc * epc, (cc + 1) * epc)], axis=1)
        acc_ref[...] = jnp.zeros(acc_ref.shape, F32)

    ge = ge_ref[c]
    y = acc_ref[...]
    for e in range(epc):
        a = _dotw(x, w1_ref[0, e])
        b = _dotw(x, w3_ref[0, e])
        h = (a * jax.nn.sigmoid(a)) * b * ge[:, e * D_EXPERT:(e + 1) * D_EXPERT]
        y = y + _dotw(h, w2_ref[0, e])
    acc_ref[...] = y

    @pl.when(c == pl.num_programs(0) - 1)
    def _():
        out_ref[...] = _ln(ALPHA * x + acc_ref[...], g_ref[...], b_ref[...])


def _moe_dec(x2d, rw, rb, w1, w3, w2, layer, g, b, epc=2):
    m, d = x2d.shape
    cw = epc * D_EXPERT
    nchunk = N_EXPERTS // epc
    const = lambda c: (0, 0)
    return pl.pallas_call(
        functools.partial(_moe_dec_body, epc=epc),
        grid=(nchunk,),
        in_specs=[pl.BlockSpec((m, d), const), pl.BlockSpec((d, N_EXPERTS), const), pl.BlockSpec((1, N_EXPERTS), const),
                  pl.BlockSpec((1, epc, d, D_EXPERT), lambda c: (layer, c, 0, 0)),
                  pl.BlockSpec((1, epc, d, D_EXPERT), lambda c: (layer, c, 0, 0)),
                  pl.BlockSpec((1, epc, D_EXPERT, d), lambda c: (layer, c, 0, 0)), pl.BlockSpec((1, d), const),
                  pl.BlockSpec((1, d), const)],
        out_specs=pl.BlockSpec((m, d), const),
        out_shape=jax.ShapeDtypeStruct((m, d), F32),
        scratch_shapes=[pltpu.VMEM((nchunk, m, cw), F32), pltpu.VMEM((m, d), F32)],
        compiler_params=_cparams(1),
        name="moe_dec",
    )(x2d, rw, rb, w1, w3, w2, g, b)


def _fox_prep_body(ff_ref, bf_ref, logf_ref, cum_ref):
    lf = jax.nn.log_sigmoid(ff_ref[0] + bf_ref[...])
    logf_ref[0] = lf
    n = lf.shape[1]
    lane = lax.broadcasted_iota(jnp.int32, lf.shape, 1)
    c = lf
    s = 1
    while s < n:
        c = c + jnp.where(lane >= s, pltpu.roll(c, s, 1), 0.0)
        s *= 2
    cum_ref[0] = c


def _fox_prep(ff_t, bf):
    bsz, nh, n = ff_t.shape
    blk = pl.BlockSpec((1, nh, n), lambda b: (b, 0, 0))
    return pl.pallas_call(
        _fox_prep_body,
        grid=(bsz,),
        in_specs=[blk, _resident((nh, 1))],
        out_specs=[blk, blk],
        out_shape=[jax.ShapeDtypeStruct((bsz, nh, n), F32)] * 2,
        compiler_params=_cparams(1),
        name="fox_prep",
    )(ff_t, bf)


def _fox_body(q_ref, kv_ref, cq_ref, ck_ref, g_ref, o_ref, m_ref, l_ref, acc_ref, *, tq):
    qi = pl.program_id(1)
    hd, nh, gw = HEAD_DIM, N_HEADS, GROUP_WIDTH
    scale = hd ** -0.5
    q = q_ref[...] * scale
    qb = [q[:, h * hd:(h + 1) * hd].astype(BF16) for h in range(nh)]
    cq = cq_ref[0]
    m_ref[...] = jnp.full(m_ref.shape, NEG, F32)
    l_ref[...] = jnp.zeros(l_ref.shape, F32)
    acc_ref[...] = jnp.zeros(acc_ref.shape, F32)
    row = lax.broadcasted_iota(jnp.int32, (tq, tq), 0)
    col = lax.broadcasted_iota(jnp.int32, (tq, tq), 1)

    def chunk(kc, diagonal):
        r0 = pl.multiple_of(kc * tq, tq)
        kv = kv_ref[pl.ds(r0, tq), :]
        ck = ck_ref[0, :, pl.ds(r0, tq)]
        for h in range(nh):
            k = kv[:, h * hd:(h + 1) * hd].astype(BF16)
            v = kv[:, gw + h * hd:gw + (h + 1) * hd].astype(BF16)
            s = _dot_nt(qb[h], k) + cq[:, h:h + 1] - ck[h:h + 1, :]
            if diagonal:
                s = jnp.where(col <= row, s, NEG)
            m_old = m_ref[h]
            m_new = jnp.maximum(m_old, jnp.max(s, axis=1, keepdims=True))
            a = jnp.exp(m_old - m_new)
            p = jnp.exp(s - m_new)
            l_ref[h] = a * l_ref[h] + jnp.sum(p, axis=1, keepdims=True)
            acc_ref[:, h * hd:(h + 1) * hd] = (a * acc_ref[:, h * hd:(h + 1) * hd]
                                               + jnp.dot(p.astype(BF16), v, preferred_element_type=F32))
            m_ref[h] = m_new

    def body(kc, carry):
        chunk(kc, False)
        return carry

    lax.fori_loop(0, qi, body, 0)
    chunk(qi, True)
    o = jnp.concatenate([acc_ref[:, h * hd:(h + 1) * hd] / jnp.maximum(l_ref[h], 1e-30) for h in range(nh)], axis=1)
    o_ref[...] = _rms(o, g_ref[...])


def _fox_prompt(proj, cq, ck, g, bsz, n, tq):
    nh, gw = N_HEADS, GROUP_WIDTH
    nq = n // tq
    return pl.pallas_call(
        functools.partial(_fox_body, tq=tq),
        grid=(bsz, nq),
        in_specs=[pl.BlockSpec((tq, gw), lambda b, i: (b * nq + i, C_FQ // gw)),
                  pl.BlockSpec((n, 2 * gw), lambda b, i: (b, C_FKV // (2 * gw))),
                  pl.BlockSpec((1, tq, nh), lambda b, i: (b, i, 0)),
                  pl.BlockSpec((1, nh, n), lambda b, i: (b, 0, 0)),
                  _resident((1, gw))],
        out_specs=pl.BlockSpec((tq, gw), lambda b, i: (b * nq + i, 0)),
        out_shape=jax.ShapeDtypeStruct((bsz * n, gw), F32),
        scratch_shapes=[pltpu.VMEM((nh, tq, 1), F32), pltpu.VMEM((nh, tq, 1), F32), pltpu.VMEM((tq, gw), F32)],
        compiler_params=_cparams(2),
        name="fox_prompt",
    )(proj, proj, cq, ck, g)


def _slope_col(slopes_ref, tq):
    hrow = lax.broadcasted_iota(jnp.int32, (N_HEADS * tq, 1), 0) // tq
    s = jnp.full((N_HEADS * tq, 1), slopes_ref[0], F32)
    for h in range(1, N_HEADS):
        s = jnp.where(hrow == h, slopes_ref[h], s)
    return s


def _softmax_rows(s, mask):
    s = jnp.where(mask, s, NEG)
    m = jnp.max(s, axis=1, keepdims=True)
    p = jnp.where(mask, jnp.exp(s - m), 0.0)
    return p / jnp.maximum(jnp.sum(p, axis=1, keepdims=True), 1e-30)


def _nsa_body(slopes_ref, q_ref, rows_ref, win_ref, small_ref, pw_ref, pair_ref, exp_ref, g_ref, o_ref,
              kvc_ref, m_ref, l_ref, acc_ref, *, tq, tk, n):
    qi = pl.program_id(1)
    hd, nh = HEAD_DIM, N_HEADS
    scale = hd ** -0.5
    n_cmp, n_sel = n // CMP_BLOCK, n // SEL_BLOCK
    r4 = nh * tq

    @pl.when(qi == 0)
    def _():
        kv = rows_ref[:, 0:2 * hd].reshape(n_cmp, CMP_BLOCK, 2 * hd)
        kvc_ref[...] = jnp.sum(kv * pw_ref[...][None], axis=1)

    q = q_ref[...] * scale
    qs = jnp.concatenate([q[:, h * hd:(h + 1) * hd] for h in range(nh)], axis=0).astype(BF16)
    slope = _slope_col(slopes_ref, tq)
    q0 = qi * tq
    t_row = q0 + lax.broadcasted_iota(jnp.int32, (r4, 1), 0) % tq
    t_q = q0 + lax.broadcasted_iota(jnp.int32, (tq, 1), 0)

    kvc = kvc_ref[...]
    s_c = _dot_nt(qs, kvc[:, 0:hd].astype(BF16))
    cmp_end = (lax.broadcasted_iota(jnp.int32, (1, n_cmp), 1) + 1) * CMP_BLOCK - 1
    dist_c = t_row - cmp_end
    p_c = _softmax_rows(s_c - slope * dist_c.astype(F32), dist_c >= 0)
    o_c = jnp.dot(p_c.astype(BF16), kvc[:, hd:2 * hd].astype(BF16), preferred_element_type=F32)

    psum = p_c[0:tq]
    for h in range(1, nh):
        psum = psum + p_c[h * tq:(h + 1) * tq]
    imp = lax.dot_general(pair_ref[...], psum, (((1,), (1,)), ((), ())), precision=HI,
                          preferred_element_type=F32)
    blk = lax.broadcasted_iota(jnp.int32, (n_sel, tq), 0)
    cur = (q0 + lax.broadcasted_iota(jnp.int32, (1, tq), 1)) // SEL_BLOCK
    valid = blk <= cur
    forced = (blk == cur) | (blk == 0)
    score = jnp.where(valid, jnp.where(forced, FORCE_SCORE, imp), -jnp.inf)
    rank = jnp.zeros((n_sel, tq), jnp.int32)
    for j in range(n_sel):
        sj = score[j:j + 1, :]
        beats = (sj > score) | ((sj == score) & (blk > j))
        rank = rank + beats.astype(jnp.int32)
    sel = ((rank < SEL_TOPK) & valid).astype(BF16)

    far = 1e9

    m_ref[...] = jnp.full(m_ref.shape, NEG, F32)
    l_ref[...] = jnp.zeros(l_ref.shape, F32)
    acc_ref[...] = jnp.zeros(acc_ref.shape, F32)
    lane_k = lax.broadcasted_iota(jnp.int32, (1, tk), 1)

    def body(kc, carry):
        r0 = pl.multiple_of(kc * tk, tk)
        kvs = rows_ref[pl.ds(r0, tk), 2 * hd:4 * hd]
        ks, vs = kvs[:, 0:hd].astype(BF16), kvs[:, hd:2 * hd].astype(BF16)
        dist = t_q - (r0 + lane_k)
        keep = _dot_tn(sel, exp_ref[kc])
        dmask = jnp.where((dist >= 0) & (keep > 0.5), dist.astype(F32), far)
        s = _dot_nt(qs, ks) - slope * jnp.concatenate([dmask] * nh, axis=0)
        m_old = m_ref[...]
        m_new = jnp.maximum(m_old, jnp.max(s, axis=1, keepdims=True))
        a = jnp.exp(m_old - m_new)
        p = jnp.exp(s - m_new)
        l_ref[...] = a * l_ref[...] + jnp.sum(p, axis=1, keepdims=True)
        acc_ref[...] = a * acc_ref[...] + jnp.dot(p.astype(BF16), vs, preferred_element_type=F32)
        m_ref[...] = m_new
        return carry

    lax.fori_loop(0, (q0 + tq + tk - 1) // tk, body, 0)
    o_s = acc_ref[...] / l_ref[...]

    wlen = WINDOW + tq
    w0 = pl.multiple_of(jnp.maximum(q0 - WINDOW, 0), tq)
    wkv = win_ref[pl.ds(w0, wlen), :]
    dist_w = t_q - (w0 + lax.broadcasted_iota(jnp.int32, (1, wlen), 1))
    dmask_w = jnp.where((dist_w >= 0) & (dist_w < WINDOW), dist_w.astype(F32), far)
    s_w = _dot_nt(qs, wkv[:, 0:hd].astype(BF16)) - slope * jnp.concatenate([dmask_w] * nh, axis=0)
    p_w = jnp.exp(s_w - jnp.max(s_w, axis=1, keepdims=True))
    o_w = (jnp.dot(p_w.astype(BF16), wkv[:, hd:2 * hd].astype(BF16), preferred_element_type=F32)
           / jnp.sum(p_w, axis=1, keepdims=True))

    gt = jax.nn.sigmoid(small_ref[:, S_NGATE:S_NGATE + 3 * nh])
    outs = []
    for h in range(nh):
        sl = slice(h * tq, (h + 1) * tq)
        outs.append(gt[:, 3 * h:3 * h + 1] * o_c[sl] + gt[:, 3 * h + 1:3 * h + 2] * o_s[sl]
                    + gt[:, 3 * h + 2:3 * h + 3] * o_w[sl])
    o_ref[...] = _rms(jnp.concatenate(outs, axis=1), g_ref[...])


def _nsa_prompt(proj, slopes, pool_w, g, bsz, n, tq=128, tk=512):
    hd, nh, gw = HEAD_DIM, N_HEADS, GROUP_WIDTH
    assert n % tk == 0 and n >= WINDOW + tq and n % SEL_BLOCK == 0
    nq = n // tq
    n_cmp, n_sel = n // CMP_BLOCK, n // SEL_BLOCK
    pw = jnp.concatenate([jnp.broadcast_to(pool_w[0][:, None], (CMP_BLOCK, hd)),
                          jnp.broadcast_to(pool_w[1][:, None], (CMP_BLOCK, hd))], axis=1)
    pair = (np.arange(n_sel)[:, None] == np.arange(n_cmp)[None, :] // 2).astype(np.float32)
    expand = (np.arange(n_sel)[:, None] == np.arange(n)[None, :] // SEL_BLOCK)
    expand = jnp.asarray(expand.reshape(n_sel, n // tk, tk).transpose(1, 0, 2), BF16)
    return pl.pallas_call(
        functools.partial(_nsa_body, tq=tq, tk=tk, n=n),
        grid=(bsz, nq),
        in_specs=[pl.BlockSpec(memory_space=pltpu.SMEM),
                  pl.BlockSpec((tq, gw), lambda b, i: (b * nq + i, C_NQ // gw)),
                  pl.BlockSpec((n, gw), lambda b, i: (b, C_NROWS // gw)),
                  pl.BlockSpec((n, 2 * hd), lambda b, i: (b, C_WIN // (2 * hd))),
                  pl.BlockSpec((tq, 128), lambda b, i: (b * nq + i, C_SMALL // 128)),
                  _resident((CMP_BLOCK, 2 * hd)), _resident((n_sel, n_cmp)), _resident((n // tk, n_sel, tk)),
                  _resident((1, gw))],
        out_specs=pl.BlockSpec((tq, gw), lambda b, i: (b * nq + i, 0)),
        out_shape=jax.ShapeDtypeStruct((bsz * n, gw), F32),
        scratch_shapes=[pltpu.VMEM((n_cmp, 2 * hd), F32), pltpu.VMEM((nh * tq, 1), F32),
                        pltpu.VMEM((nh * tq, 1), F32), pltpu.VMEM((nh * tq, hd), F32)],
        compiler_params=_cparams(2),
        name="nsa_prompt",
    )(slopes, proj, proj, proj, proj, pw, jnp.asarray(pair), expand, g)


def _s5_body(u_ref, bre_ref, bim_ref, are_ref, aim_ref, s0r_ref, s0i_ref, cre_ref, cim_ref, d_ref, wg_ref, g_ref,
             o_ref, fr_ref, fi_ref, xr_ref, xi_ref, sr_ref, si_ref, *, bsz, tc):
    @pl.when(pl.program_id(0) == 0)
    def _():
        sr_ref[...] = s0r_ref[...]
        si_ref[...] = s0i_ref[...]

    u = u_ref[...]
    xr_ref[...] = _dotw(u, bre_ref[...])
    xi_ref[...] = _dotw(u, bim_ref[...])
    ar = jnp.broadcast_to(are_ref[...], sr_ref.shape)
    ai = jnp.broadcast_to(aim_ref[...], sr_ref.shape)

    def step(t, carry):
        sr, si = carry
        r0 = pl.multiple_of(t * bsz, bsz)
        nr = ar * sr - ai * si + xr_ref[pl.ds(r0, bsz), :]
        ni = ar * si + ai * sr + xi_ref[pl.ds(r0, bsz), :]
        xr_ref[pl.ds(r0, bsz), :] = nr
        xi_ref[pl.ds(r0, bsz), :] = ni
        return nr, ni

    sr, si = lax.fori_loop(0, tc, step, (sr_ref[...], si_ref[...]))
    sr_ref[...] = sr
    si_ref[...] = si
    fr_ref[...] = sr
    fi_ref[...] = si
    y = _dotw(xr_ref[...], cre_ref[...]) - _dotw(xi_ref[...], cim_ref[...])
    y = jax.nn.gelu(y + d_ref[...] * u)
    out = y * jax.nn.sigmoid(_dotw(y, wg_ref[...]))
    o_ref[...] = _rms(out, g_ref[...])


def _s5(u_tm, prm, s0r, s0i, g, bsz, n, tc):
    gw = GROUP_WIDTH
    ns = S5_GROUPS * S5_STATE
    rows = tc * bsz
    st = jax.ShapeDtypeStruct((bsz, ns), F32)
    return pl.pallas_call(
        functools.partial(_s5_body, bsz=bsz, tc=tc),
        grid=(n // tc,),
        in_specs=[pl.BlockSpec((rows, gw), lambda i: (i, 0)),
                  _resident((gw, ns)), _resident((gw, ns)), _resident((1, ns)), _resident((1, ns)),
                  _resident((bsz, ns)), _resident((bsz, ns)), _resident((ns, gw)), _resident((ns, gw)),
                  _resident((1, gw)), _resident((gw, gw)), _resident((1, gw))],
        out_specs=[pl.BlockSpec((rows, gw), lambda i: (i, 0)), pl.BlockSpec((bsz, ns), lambda i: (0, 0)),
                   pl.BlockSpec((bsz, ns), lambda i: (0, 0))],
        out_shape=[jax.ShapeDtypeStruct((n * bsz, gw), F32), st, st],
        scratch_shapes=[pltpu.VMEM((rows, ns), F32), pltpu.VMEM((rows, ns), F32),
                        pltpu.VMEM((bsz, ns), F32), pltpu.VMEM((bsz, ns), F32)],
        compiler_params=_cparams(1),
        name="s5",
    )(u_tm, prm["bre"], prm["bim"], prm["are"], prm["aim"], s0r, s0i, prm["cre"], prm["cim"], prm["d"], prm["wg"], g)


def _s5_params(a_re, a_im, b_re, b_im, c_re, c_im, d, log_dt, w_glu, wdt):
    dt = jnp.exp(log_dt)[:, None]
    mag = jnp.exp(dt * a_re)
    abar_re, abar_im = mag * jnp.cos(dt * a_im), mag * jnp.sin(dt * a_im)
    den = a_re * a_re + a_im * a_im
    num_re, num_im = abar_re - 1.0, abar_im
    zoh_re = (num_re * a_re + num_im * a_im) / den
    zoh_im = (num_im * a_re - num_re * a_im) / den
    bbar_re = zoh_re[..., None] * b_re - zoh_im[..., None] * b_im
    bbar_im = zoh_re[..., None] * b_im + zoh_im[..., None] * b_re
    eye = jnp.eye(S5_GROUPS, dtype=F32)
    ns = S5_GROUPS * S5_STATE
    to_in = lambda w: (w[:, :, None, :] * eye[:, None, :, None]).transpose(0, 3, 2, 1).reshape(GROUP_WIDTH, ns).astype(wdt)
    to_out = lambda w: (w[:, :, None, :] * eye[:, None, :, None]).transpose(0, 3, 2, 1).reshape(ns, GROUP_WIDTH).astype(wdt)
    return {"bre": to_in(bbar_re), "bim": to_in(bbar_im), "are": abar_re.reshape(1, ns), "aim": abar_im.reshape(1, ns),
            "cre": to_out(c_re), "cim": to_out(c_im), "d": d.reshape(1, GROUP_WIDTH), "wg": w_glu.astype(wdt)}


def _mm_bf16(a, b):
    return jnp.dot(a.astype(BF16), b.astype(BF16), preferred_element_type=F32)


def _l2n(x):
    return x * lax.rsqrt(jnp.sum(x * x, axis=-1, keepdims=True) + RMS_EPS)


def _gdn_body(qkv_ref, z_ref, small_ref, at_ref, cb_ref, s0_ref, cw_ref, alr_ref, dtr_ref, alc_ref, dtc_ref, ng_ref,
              o_ref, sfin_ref, ext_ref, s_ref, *, tc):
    j = pl.program_id(1)
    hd, nh, gw, ch = HEAD_DIM, N_HEADS, GROUP_WIDTH, GDN_CHUNK
    pad = 8

    @pl.when(j == 0)
    def _():
        ext_ref[0:pad, :] = cb_ref[0]
        s_ref[...] = s0_ref[0]

    @pl.when(j > 0)
    def _():
        ext_ref[0:pad, :] = ext_ref[tc:tc + pad, :]

    ext_ref[pad:tc + pad, :] = qkv_ref[...]
    base = pad - (CONV_W - 1)
    conv = ext_ref[base:base + tc, :] * cw_ref[0:1, :]
    for t in range(1, CONV_W):
        conv = conv + ext_ref[base + t:base + t + tc, :] * cw_ref[t:t + 1, :]
    conv = conv * jax.nn.sigmoid(conv)

    small = small_ref[...]
    g_col = -jnp.exp(alr_ref[...]) * jax.nn.softplus(small[:, S_GA:S_GA + nh] + dtr_ref[...])
    beta = jax.nn.sigmoid(small[:, S_GB:S_GB + nh])
    g_row = -jnp.exp(alc_ref[...]) * jax.nn.softplus(at_ref[0][0:nh, :] + dtc_ref[...])
    ri = lax.broadcasted_iota(jnp.int32, (tc, tc), 0)
    ci = lax.broadcasted_iota(jnp.int32, (tc, tc), 1)
    same = (ri // ch) == (ci // ch)
    tril = same & (ci <= ri)
    strict = same & (ci < ri)
    tril_f = tril.astype(F32)
    cum_col = jnp.dot(tril_f, g_col, precision=HI, preferred_element_type=F32)
    cum_row = lax.dot_general(g_row, tril_f, (((1,), (1,)), ((), ())), precision=HI,
                              preferred_element_type=F32)
    z = z_ref[...]
    outs = []
    for h in range(nh):
        gc = cum_col[:, h:h + 1]
        decay = jnp.where(tril, jnp.exp(jnp.where(tril, gc - cum_row[h:h + 1, :], 0.0)), 0.0)
        q = _l2n(conv[:, h * hd:(h + 1) * hd]) * hd ** -0.5
        k = _l2n(conv[:, gw + h * hd:gw + (h + 1) * hd])
        v = conv[:, 2 * gw + h * hd:2 * gw + (h + 1) * hd]
        bc = beta[:, h:h + 1]
        kb = k * bc
        kbb, kbf = kb.astype(BF16), k.astype(BF16)
        m = jnp.where(strict, _dot_nt(kbb, kbf) * decay, 0.0)
        nmat = -m
        qm = m
        for _ in range(int(math.log2(ch)) - 1):
            qm = _mm_bf16(qm, qm)
            nmat = nmat + qm + _mm_bf16(nmat, qm)
        eg = jnp.exp(gc)
        rhs = jnp.concatenate([v * bc, kb * eg], axis=1)
        uw = rhs + _mm_bf16(nmat, rhs)
        u, w = uw[:, 0:hd], uw[:, hd:2 * hd]
        attn = (_dot_nt(q.astype(BF16), kbf) * decay).astype(BF16)
        qg = (q * eg).astype(BF16)
        s = s_ref[h]
        o_chunks = []
        for c in range(tc // ch):
            r = slice(c * ch, (c + 1) * ch)
            sb = s.astype(BF16)
            v_new = u[r] - jnp.dot(w[r].astype(BF16), sb, preferred_element_type=F32)
            vb = v_new.astype(BF16)
            o_chunks.append(jnp.dot(qg[r], sb, preferred_element_type=F32)
                            + jnp.dot(attn[r, r], vb, preferred_element_type=F32))
            g_last = gc[(c + 1) * ch - 1:(c + 1) * ch, :]
            kd = (k[r] * jnp.exp(g_last - gc[r])).astype(BF16)
            s = s * jnp.exp(g_last) + _dot_tn(kd, vb)
        s_ref[h] = s
        o = jnp.concatenate(o_chunks, axis=0)
        o = o * lax.rsqrt(jnp.mean(o * o, axis=-1, keepdims=True) + RMS_EPS) * ng_ref[:, h * hd:(h + 1) * hd]
        zh = z[:, h * hd:(h + 1) * hd]
        outs.append(o * (zh * jax.nn.sigmoid(zh)))
    o_ref[...] = jnp.concatenate(outs, axis=1)
    sfin_ref[0] = s_ref[...]


def _gdn_prompt(proj, a_t, conv_buf8, s0, conv_w, a_log, dt_bias, norm_g, bsz, n, tc=256):
    hd, nh, gw = HEAD_DIM, N_HEADS, GROUP_WIDTH
    nb = n // tc
    return pl.pallas_call(
        functools.partial(_gdn_body, tc=tc),
        grid=(bsz, nb),
        in_specs=[pl.BlockSpec((tc, 3 * gw), lambda b, j: (b * nb + j, C_GQKV // (3 * gw))),
                  pl.BlockSpec((tc, gw), lambda b, j: (b * nb + j, C_GZ // gw)),
                  pl.BlockSpec((tc, 128), lambda b, j: (b * nb + j, C_SMALL // 128)),
                  pl.BlockSpec((1, 8, tc), lambda b, j: (b, 0, j)),
                  pl.BlockSpec((1, 8, 3 * gw), lambda b, j: (b, 0, 0)),
                  pl.BlockSpec((1, nh, hd, hd), lambda b, j: (b, 0, 0, 0)),
                  _resident((CONV_W, 3 * gw)), _resident((1, nh)), _resident((1, nh)), _resident((nh, 1)),
                  _resident((nh, 1)), _resident((1, gw))],
        out_specs=[pl.BlockSpec((tc, gw), lambda b, j: (b * nb + j, 0)),
                   pl.BlockSpec((1, nh, hd, hd), lambda b, j: (b, 0, 0, 0))],
        out_shape=[jax.ShapeDtypeStruct((bsz * n, gw), F32), jax.ShapeDtypeStruct((bsz, nh, hd, hd), F32)],
        scratch_shapes=[pltpu.VMEM((tc + 8, 3 * gw), F32), pltpu.VMEM((nh, hd, hd), F32)],
        compiler_params=_cparams(2),
        name="gdn_prompt",
    )(proj, proj, proj, a_t, conv_buf8, s0, conv_w, a_log.reshape(1, nh), dt_bias.reshape(1, nh),
      a_log.reshape(nh, 1), dt_bias.reshape(nh, 1), norm_g)


def _head_ones():
    h = np.arange(GROUP_WIDTH) // HEAD_DIM
    return jnp.asarray((h[:, None] == h[None, :]).astype(np.float32))


def _head_scores(k4, q, ones):
    return jnp.dot(k4 * q, ones, precision=HI, preferred_element_type=F32) * HEAD_DIM ** -0.5


def _tile4(x):
    return jnp.concatenate([x] * N_HEADS, axis=1)


def _fox_dec_body(pt_ref, q_ref, kvn_ref, small_ref, bf_ref, eff_ref, ones_ref, g_ref, *rest, npg, nstep):
    kv_refs, lf_refs = rest[0:npg], rest[npg:2 * npg]
    o_ref, lfo_ref, m_ref, l_ref, acc_ref, car_ref = rest[2 * npg:]
    j = pl.program_id(1)
    gw, hd, nh = GROUP_WIDTH, HEAD_DIM, N_HEADS
    q = q_ref[0]
    ones = ones_ref[...]

    @pl.when(j == 0)
    def _():
        lf_small = jax.nn.log_sigmoid(small_ref[0] + bf_ref[...])
        lfo_ref[0] = lf_small
        kvn = kvn_ref[0]
        m_ref[...] = _head_scores(kvn[:, 0:gw], q, ones)
        l_ref[...] = jnp.ones(l_ref.shape, F32)
        acc_ref[...] = kvn[:, gw:2 * gw]
        car_ref[...] = jnp.dot(lf_small, eff_ref[...], precision=HI, preferred_element_type=F32)

    ri = lax.broadcasted_iota(jnp.int32, (128, 128), 0)
    ci = lax.broadcasted_iota(jnp.int32, (128, 128), 1)
    upper = (ci > ri).astype(F32)
    m, l, acc, car = m_ref[...], l_ref[...], acc_ref[...], car_ref[...]
    for i in range(npg):
        kv = kv_refs[i][0, 0]
        lf = lf_refs[i][0, 0]
        lfe = jnp.concatenate([jnp.broadcast_to(lf[:, h:h + 1], (lf.shape[0], hd)) for h in range(nh)], axis=1)
        suf = jnp.dot(upper, lfe, precision=HI, preferred_element_type=F32)
        s = _head_scores(kv[:, 0:gw], q, ones) + car + suf
        m_new = jnp.maximum(m, jnp.max(s, axis=0, keepdims=True))
        a = jnp.exp(m - m_new)
        p = jnp.exp(s - m_new)
        l = a * l + jnp.sum(p, axis=0, keepdims=True)
        acc = a * acc + jnp.sum(p * kv[:, gw:2 * gw], axis=0, keepdims=True)
        m = m_new
        car = car + jnp.sum(lfe, axis=0, keepdims=True)
    m_ref[...], l_ref[...], acc_ref[...], car_ref[...] = m, l, acc, car

    @pl.when(j == nstep - 1)
    def _():
        o_ref[0] = _rms(acc / l, g_ref[...])


def _fox_decode(proj3, page_table, kv_cache, lf_cache, layer, bf_pad, g, npg=8):
    bsz = proj3.shape[0]
    n_pages = page_table.shape[1]
    gw, nh = GROUP_WIDTH, N_HEADS
    nstep = n_pages // npg
    eff = np.zeros((128, gw), np.float32)
    for h in range(nh):
        eff[S_FF + h, h * HEAD_DIM:(h + 1) * HEAD_DIM] = 1.0
    page = lambda i: (lambda b, j, pt: (layer, pt[b, n_pages - 1 - (j * npg + i)], 0, 0))
    const2 = lambda b, j, pt: (0, 0)
    grid_spec = pltpu.PrefetchScalarGridSpec(
        num_scalar_prefetch=1,
        grid=(bsz, nstep),
        in_specs=[pl.BlockSpec((1, 1, gw), lambda b, j, pt: (b, 0, C_FQ // gw)),
                  pl.BlockSpec((1, 1, 2 * gw), lambda b, j, pt: (b, 0, C_FKV // (2 * gw))),
                  pl.BlockSpec((1, 1, 128), lambda b, j, pt: (b, 0, C_SMALL // 128)),
                  pl.BlockSpec((1, 128), const2), pl.BlockSpec((128, gw), const2), pl.BlockSpec((gw, gw), const2),
                  pl.BlockSpec((1, gw), const2)]
        + [pl.BlockSpec((1, 1, 128, 2 * gw), page(i)) for i in range(npg)]
        + [pl.BlockSpec((1, 1, 128, nh), page(i)) for i in range(npg)],
        out_specs=[pl.BlockSpec((1, 1, gw), lambda b, j, pt: (b, 0, 0)),
                   pl.BlockSpec((1, 1, 128), lambda b, j, pt: (b, 0, 0))],
        scratch_shapes=[pltpu.VMEM((1, gw), F32)] * 4,
    )
    return pl.pallas_call(
        functools.partial(_fox_dec_body, npg=npg, nstep=nstep),
        grid_spec=grid_spec,
        out_shape=[jax.ShapeDtypeStruct((bsz, 1, gw), F32), jax.ShapeDtypeStruct((bsz, 1, 128), F32)],
        compiler_params=_cparams(2),
        name="fox_decode",
    )(page_table, proj3, proj3, proj3, bf_pad, jnp.asarray(eff), _head_ones(), g,
      *([kv_cache] * npg), *([lf_cache] * npg))


def _nsa_dec_cmp_body(pt_ref, q_ref, pw_ref, slope_ref, ones_ref, *rest, npg, nstep, p0):
    pg_refs = rest[0:npg]
    oc_ref, idx_ref, kvc_ref, imp_ref = rest[npg:]
    j = pl.program_id(1)
    gw, hd, nh = GROUP_WIDTH, HEAD_DIM, N_HEADS
    per_page = 128 // CMP_BLOCK
    pw = pw_ref[...]
    pooled = [jnp.sum((pg_refs[i][0, 0] * pw).reshape(per_page, CMP_BLOCK, 2 * hd), axis=1) for i in range(npg)]
    r0 = pl.multiple_of(j * (npg * per_page), npg * per_page)
    kvc_ref[pl.ds(r0, npg * per_page), :] = jnp.concatenate(pooled, axis=0)

    @pl.when(j == nstep - 1)
    def _():
        n_cmp = kvc_ref.shape[0]
        n_sel = n_cmp // 2
        q = q_ref[0]
        kvc = kvc_ref[...]
        cmp_end = (lax.broadcasted_iota(jnp.int32, (n_cmp, 1), 0) + 1) * CMP_BLOCK - 1
        dist = p0 - cmp_end
        s = _head_scores(_tile4(kvc[:, 0:hd]), q, ones_ref[...]) - slope_ref[...] * dist.astype(F32)
        ok = dist >= 0
        s = jnp.where(ok, s, NEG)
        p = jnp.where(ok, jnp.exp(s - jnp.max(s, axis=0, keepdims=True)), 0.0)
        p = p / jnp.maximum(jnp.sum(p, axis=0, keepdims=True), 1e-30)
        oc_ref[0] = jnp.sum(p * _tile4(kvc[:, hd:2 * hd]), axis=0, keepdims=True)
        imp = p[:, 0:1]
        for h in range(1, nh):
            imp = imp + p[:, h * hd:h * hd + 1]
        imp_ref[...] = jnp.broadcast_to(imp, imp_ref.shape)
        colm = imp_ref[pl.ds(0, n_sel, stride=2), :] + imp_ref[pl.ds(1, n_sel, stride=2), :]
        ri = lax.broadcasted_iota(jnp.int32, (n_sel, n_sel), 0)
        ci = lax.broadcasted_iota(jnp.int32, (n_sel, n_sel), 1)
        colm = jnp.where(ri == 0, FORCE_SCORE, colm)
        rowm = colm.T
        beats = (rowm > colm) | ((rowm == colm) & (ci < ri))
        rank = jnp.sum(beats.astype(F32), axis=1, keepdims=True)
        sel = (rank < SEL_TOPK - 1).astype(F32)
        pos = jnp.dot((ci < ri).astype(F32), jnp.broadcast_to(sel, (n_sel, n_sel)), preferred_element_type=F32)
        onehot = jnp.where((sel > 0.5) & (pos == ci.astype(F32)), ri.astype(F32), 0.0)
        idx_ref[0] = jnp.sum(onehot, axis=0, keepdims=True).astype(jnp.int32)


def _nsa_dec_cmp(proj3, page_table, cache2, layer, pw128, slope_e, p0, npg=16):
    bsz = proj3.shape[0]
    n_pages = page_table.shape[1]
    gw, hd = GROUP_WIDTH, HEAD_DIM
    nstep = n_pages // npg
    n_cmp = p0 // CMP_BLOCK
    assert p0 % SEL_BLOCK == 0 and n_cmp // 2 == 128 and p0 == n_pages * 128
    page = lambda i: (lambda b, j, pt: (layer, pt[b, j * npg + i], 0, 0))
    const2 = lambda b, j, pt: (0, 0)
    grid_spec = pltpu.PrefetchScalarGridSpec(
        num_scalar_prefetch=1,
        grid=(bsz, nstep),
        in_specs=[pl.BlockSpec((1, 1, gw), lambda b, j, pt: (b, 0, C_NQ // gw)),
                  pl.BlockSpec((128, 2 * hd), const2), pl.BlockSpec((1, gw), const2), pl.BlockSpec((gw, gw), const2)]
        + [pl.BlockSpec((1, 1, 128, 2 * hd), page(i)) for i in range(npg)],
        out_specs=[pl.BlockSpec((1, 1, gw), lambda b, j, pt: (b, 0, 0)),
                   pl.BlockSpec((1, 1, 128), lambda b, j, pt: (b, 0, 0))],
        scratch_shapes=[pltpu.VMEM((n_cmp, 2 * hd), F32), pltpu.VMEM((n_cmp, 128), F32)],
    )
    return pl.pallas_call(
        functools.partial(_nsa_dec_cmp_body, npg=npg, nstep=nstep, p0=p0),
        grid_spec=grid_spec,
        out_shape=[jax.ShapeDtypeStruct((bsz, 1, gw), F32), jax.ShapeDtypeStruct((bsz, 1, 128), jnp.int32)],
        compiler_params=_cparams(2),
        name="nsa_dec_cmp",
    )(page_table, proj3, pw128, slope_e, _head_ones(), *([cache2] * npg))


def _nsa_dec_sel_body(pt_ref, idx_ref, q_ref, new_ref, wnew_ref, small_ref, oc_ref, win_ref, slope_ref, ones_ref,
                      eg_ref, g_ref, *rest, nsel, p0):
    blk_refs = rest[0:nsel]
    o_ref = rest[nsel]
    b = pl.program_id(0)
    hd = HEAD_DIM
    q = q_ref[0]
    ones = ones_ref[...]
    slope = slope_ref[...]
    new = new_ref[0]
    s_new = _head_scores(_tile4(new[:, 2 * hd:3 * hd]), q, ones)
    v_new = _tile4(new[:, 3 * hd:4 * hd])
    ss, vs = [], []
    r = lax.broadcasted_iota(jnp.int32, (SEL_BLOCK, 1), 0)
    for i in range(nsel):
        blk = blk_refs[i][0, 0]
        dist = (p0 - idx_ref[b, i] * SEL_BLOCK) - r
        ss.append(_head_scores(_tile4(blk[:, 0:hd]), q, ones) - slope * dist.astype(F32))
        vs.append(_tile4(blk[:, hd:2 * hd]))
    s = jnp.concatenate(ss, axis=0)
    v = jnp.concatenate(vs, axis=0)
    m = jnp.maximum(jnp.max(s, axis=0, keepdims=True), s_new)
    p, p_new = jnp.exp(s - m), jnp.exp(s_new - m)
    o_s = (jnp.sum(p * v, axis=0, keepdims=True) + p_new * v_new) / (jnp.sum(p, axis=0, keepdims=True) + p_new)
    win = win_ref[0, 0]
    wn = wnew_ref[0]
    nw = win.shape[0]
    dist_w = nw - lax.broadcasted_iota(jnp.int32, (nw, 1), 0)
    ok = dist_w < WINDOW
    s_w = jnp.where(ok, _head_scores(_tile4(win[:, 0:hd]), q, ones) - slope * dist_w.astype(F32), NEG)
    sw_new = _head_scores(_tile4(wn[:, 0:hd]), q, ones)
    mw = jnp.maximum(jnp.max(s_w, axis=0, keepdims=True), sw_new)
    pw_, pw_new = jnp.where(ok, jnp.exp(s_w - mw), 0.0), jnp.exp(sw_new - mw)
    o_w = ((jnp.sum(pw_ * _tile4(win[:, hd:2 * hd]), axis=0, keepdims=True) + pw_new * _tile4(wn[:, hd:2 * hd]))
           / (jnp.sum(pw_, axis=0, keepdims=True) + pw_new))
    gt = jax.nn.sigmoid(small_ref[0])
    ge = [jnp.dot(gt, eg_ref[c], precision=HI, preferred_element_type=F32) for c in range(3)]
    o_ref[0] = _rms(ge[0] * oc_ref[0] + ge[1] * o_s + ge[2] * o_w, g_ref[...])


def _nsa_dec_sel(proj3, page_table, idx, o_c, cache, win_state, layer, slope_e, g, p0):
    bsz = proj3.shape[0]
    gw, hd, nh = GROUP_WIDTH, HEAD_DIM, N_HEADS
    nsel = SEL_TOPK - 1
    nw = win_state.shape[2]
    eg = np.zeros((3, 128, gw), np.float32)
    for c in range(3):
        for h in range(nh):
            eg[c, S_NGATE + 3 * h + c, h * hd:(h + 1) * hd] = 1.0
    n_pool = cache.shape[1]
    per_page = cache.shape[2] // SEL_BLOCK

    def blk(i):
        def index_map(b, pt, ix):
            sel = jnp.clip(ix[b, i], 0, p0 // SEL_BLOCK - 1)
            page = jnp.clip(pt[b, sel // per_page], 0, n_pool - 1)
            return (layer, page, sel % per_page, 1)
        return index_map
    const2 = lambda b, pt, ix: (0, 0)
    grid_spec = pltpu.PrefetchScalarGridSpec(
        num_scalar_prefetch=2,
        grid=(bsz,),
        in_specs=[pl.BlockSpec((1, 1, gw), lambda b, pt, ix: (b, 0, C_NQ // gw)),
                  pl.BlockSpec((1, 1, gw), lambda b, pt, ix: (b, 0, C_NROWS // gw)),
                  pl.BlockSpec((1, 1, 2 * hd), lambda b, pt, ix: (b, 0, C_WIN // (2 * hd))),
                  pl.BlockSpec((1, 1, 128), lambda b, pt, ix: (b, 0, C_SMALL // 128)),
                  pl.BlockSpec((1, 1, gw), lambda b, pt, ix: (b, 0, 0)),
                  pl.BlockSpec((1, 1, nw, 2 * hd), lambda b, pt, ix: (layer, b, 0, 0)),
                  pl.BlockSpec((1, gw), const2), pl.BlockSpec((gw, gw), const2),
                  pl.BlockSpec((3, 128, gw), lambda b, pt, ix: (0, 0, 0)), pl.BlockSpec((1, gw), const2)]
        + [pl.BlockSpec((1, 1, SEL_BLOCK, 2 * hd), blk(i)) for i in range(nsel)],
        out_specs=pl.BlockSpec((1, 1, gw), lambda b, pt, ix: (b, 0, 0)),
    )
    return pl.pallas_call(
        functools.partial(_nsa_dec_sel_body, nsel=nsel, p0=p0),
        grid_spec=grid_spec,
        out_shape=jax.ShapeDtypeStruct((bsz, 1, gw), F32),
        compiler_params=_cparams(1),
        name="nsa_dec_sel",
    )(page_table, idx, proj3, proj3, proj3, proj3, o_c, win_state, slope_e, _head_ones(), jnp.asarray(eg), g,
      *([cache] * nsel))


def _gdn_dec_body(qkv_ref, z_ref, small_ref, cb_ref, s0_ref, cw_ref, al_ref, dt_ref, ng_ref, o_ref, s_ref):
    hd, nh, gw = HEAD_DIM, N_HEADS, GROUP_WIDTH
    cb = cb_ref[0, 0]
    conv = cb[0:1] * cw_ref[0:1, :]
    for t in range(1, CONV_W - 1):
        conv = conv + cb[t:t + 1] * cw_ref[t:t + 1, :]
    conv = conv + qkv_ref[0] * cw_ref[CONV_W - 1:CONV_W, :]
    conv = conv * jax.nn.sigmoid(conv)
    small = small_ref[0]
    g = -jnp.exp(al_ref[...]) * jax.nn.softplus(small[:, S_GA:S_GA + nh] + dt_ref[...])
    beta = jax.nn.sigmoid(small[:, S_GB:S_GB + nh])
    z = z_ref[0]
    ri = lax.broadcasted_iota(jnp.int32, (hd, hd), 0)
    ci = lax.broadcasted_iota(jnp.int32, (hd, hd), 1)
    outs = []
    for h in range(nh):
        q = _l2n(conv[:, h * hd:(h + 1) * hd]) * hd ** -0.5
        k = _l2n(conv[:, gw + h * hd:gw + (h + 1) * hd])
        v = conv[:, 2 * gw + h * hd:2 * gw + (h + 1) * hd]
        eg = jnp.exp(g[:, h:h + 1])
        bc = beta[:, h:h + 1]
        s = s0_ref[0, 0, h]
        v_new = v * bc - jnp.dot(k * bc * eg, s, precision=HI, preferred_element_type=F32)
        o = (jnp.dot(q * eg, s, precision=HI, preferred_element_type=F32)
             + jnp.sum(q * k, axis=-1, keepdims=True) * v_new)
        k_col = jnp.sum(jnp.where(ri == ci, jnp.broadcast_to(k, (hd, hd)), 0.0), axis=1, keepdims=True)
        s_ref[0, h] = s * eg + k_col * v_new
        o = o * lax.rsqrt(jnp.mean(o * o, axis=-1, keepdims=True) + RMS_EPS) * ng_ref[:, h * hd:(h + 1) * hd]
        zh = z[:, h * hd:(h + 1) * hd]
        outs.append(o * (zh * jax.nn.sigmoid(zh)))
    o_ref[0] = jnp.concatenate(outs, axis=1)


def _gdn_decode(proj3, conv_buf, s0, layer, conv_w, a_log, dt_bias, norm_g):
    bsz = proj3.shape[0]
    hd, nh, gw = HEAD_DIM, N_HEADS, GROUP_WIDTH
    const2 = lambda b: (0, 0)
    return pl.pallas_call(
        _gdn_dec_body,
        grid=(bsz,),
        in_specs=[pl.BlockSpec((1, 1, 3 * gw), lambda b: (b, 0, C_GQKV // (3 * gw))),
                  pl.BlockSpec((1, 1, gw), lambda b: (b, 0, C_GZ // gw)),
                  pl.BlockSpec((1, 1, 128), lambda b: (b, 0, C_SMALL // 128)),
                  pl.BlockSpec((1, 1, CONV_W - 1, 3 * gw), lambda b: (layer, b, 0, 0)),
                  pl.BlockSpec((1, 1, nh, hd, hd), lambda b: (layer, b, 0, 0, 0)),
                  pl.BlockSpec((CONV_W, 3 * gw), const2), pl.BlockSpec((1, nh), const2), pl.BlockSpec((1, nh), const2),
                  pl.BlockSpec((1, gw), const2)],
        out_specs=[pl.BlockSpec((1, 1, gw), lambda b: (b, 0, 0)), pl.BlockSpec((1, nh, hd, hd), lambda b: (b, 0, 0, 0))],
        out_shape=[jax.ShapeDtypeStruct((bsz, 1, gw), F32), jax.ShapeDtypeStruct((bsz, nh, hd, hd), F32)],
        compiler_params=_cparams(1),
        name="gdn_decode",
    )(proj3, proj3, proj3, conv_buf, s0, conv_w, a_log.reshape(1, nh), dt_bias.reshape(1, nh), norm_g)


def kernel(x_prompt, x_sample, cache_nsa_kv, cache_fox_kv, cache_fox_logf, state_nsa_win, state_gdn, state_gdn_conv, state_s5_re, state_s5_im, page_table, w_in, nsa_pool, s5_a_re, s5_a_im, s5_b_re, s5_b_im, s5_c_re, s5_c_im, s5_d, s5_log_dt, s5_w_glu, gdn_conv, gdn_a_log, gdn_dt_bias, fox_b_f, mix_norm, w_out, ln1_g, ln1_b, ln2_g, ln2_b, router_w, router_b, exp_w1, exp_w3, exp_w2):
    B, L, D = x_prompt.shape
    BS = x_sample.shape[0]
    hd, nh, gw = HEAD_DIM, N_HEADS, GROUP_WIDTH
    ns = S5_GROUPS * S5_STATE
    n_pool = cache_nsa_kv.shape[1]
    p0 = page_table.shape[1] * cache_nsa_kv.shape[2]
    slopes = 2.0 ** (-8.0 * (jnp.arange(nh, dtype=F32) + 1.0) / nh)
    slope_e = jnp.repeat(slopes, hd).reshape(1, gw)
    xp = x_prompt.reshape(B * L, D)
    xs = x_sample.reshape(BS, D)
    acc_p = [[] for _ in range(8)]
    acc_s = [[] for _ in range(8)]
    nsa_cache = cache_nsa_kv.reshape(DEPTH, n_pool, -1, 4 * hd)
    fox_cache = cache_fox_kv.reshape(DEPTH, n_pool, -1, 2 * gw)
    win_all = state_nsa_win.reshape(DEPTH, BS, -1, 2 * hd)
    for l in range(DEPTH):
        wp_f = _permute_w_in(w_in[l])
        wo_f = w_out[l]
        wp, wo = wp_f.astype(BF16), wo_f.astype(BF16)
        w1 = exp_w1[l].astype(BF16).transpose(1, 0, 2).reshape(D, -1)
        w3 = exp_w3[l].astype(BF16).transpose(1, 0, 2).reshape(D, -1)
        w2 = exp_w2[l].astype(BF16).reshape(-1, D)
        rb = router_b.reshape(1, -1)
        ln1 = (ln1_g[l].reshape(1, D), ln1_b[l].reshape(1, D))
        ln2 = (ln2_g[l].reshape(1, D), ln2_b[l].reshape(1, D))
        g_nsa, g_s5, g_gdn, g_fox = [g.reshape(1, gw) for g in jnp.split(mix_norm[l], N_MIXERS)]
        s5_args = (s5_a_re[l], s5_a_im[l], s5_b_re[l], s5_b_im[l], s5_c_re[l], s5_c_im[l], s5_d[l],
                   s5_log_dt[l], s5_w_glu[l])
        prm, prm_f = _s5_params(*s5_args, BF16), _s5_params(*s5_args, F32)

        proj = _in_proj(xp, wp, 256)
        o_nsa = _nsa_prompt(proj, slopes, nsa_pool[l], g_nsa, B, L)
        small = proj[:, C_SMALL:C_SMALL + 128].reshape(B, L, 128)
        logf_t, cum_t = _fox_prep(small[:, :, S_FF:S_FF + nh].transpose(0, 2, 1), fox_b_f[l].reshape(nh, 1))
        o_fox = _fox_prompt(proj, cum_t.transpose(0, 2, 1), cum_t, g_fox, B, L, 256)
        u_tm = proj[:, C_S5U:C_S5U + gw].reshape(B, L, gw).transpose(1, 0, 2).reshape(L * B, gw)
        zst = jnp.zeros((B, ns), F32)
        o_s5, s5r, s5i = _s5(u_tm, prm, zst, zst, g_s5, B, L, 256)
        o_s5 = o_s5.reshape(L, B, gw).transpose(1, 0, 2).reshape(B * L, gw)
        a_t = jnp.pad(small[:, :, S_GA:S_GA + nh].transpose(0, 2, 1), ((0, 0), (0, 8 - nh), (0, 0)))
        o_gdn, gdn_st = _gdn_prompt(proj, a_t, jnp.zeros((B, 8, 3 * gw), F32), jnp.zeros((B, nh, hd, hd), F32),
                                    gdn_conv[l], gdn_a_log[l], gdn_dt_bias[l], g_gdn, B, L)
        xp = _out_proj_ln(xp, [o_nsa, o_s5, o_gdn, o_fox], wo, *ln1, 256)
        xp = _moe_ln(xp, router_w, rb, w1, w3, w2, *ln2, 256)
        st = (proj[:, C_NROWS:C_NROWS + gw].reshape(B, L, 4, hd),
              proj[:, C_FKV:C_FKV + 2 * gw].reshape(B, L, 2, nh, hd),
              logf_t.transpose(0, 2, 1),
              proj[:, C_WIN:C_WIN + 2 * hd].reshape(B, L, 2, hd)[:, L - min(WINDOW, L):],
              gdn_st,
              proj[:, C_GQKV:C_GQKV + 3 * gw].reshape(B, L, 3 * gw)[:, L - (CONV_W - 1):],
              s5r.reshape(B, S5_GROUPS, S5_STATE), s5i.reshape(B, S5_GROUPS, S5_STATE))
        for a, v in zip(acc_p, st):
            a.append(v)

        proj_s = _in_proj(xs, wp_f, BS)
        proj3 = proj_s.reshape(BS, 1, PROJ_COLS)
        pw = jnp.concatenate([jnp.broadcast_to(nsa_pool[l][0][:, None], (CMP_BLOCK, hd)),
                              jnp.broadcast_to(nsa_pool[l][1][:, None], (CMP_BLOCK, hd))], axis=1)
        o_c, idx = _nsa_dec_cmp(proj3, page_table, nsa_cache, l, jnp.tile(pw, (128 // CMP_BLOCK, 1)), slope_e, p0)
        win_state = state_nsa_win[l]
        bf_pad = jnp.zeros((1, 128), F32).at[0, S_FF:S_FF + nh].set(fox_b_f[l])
        o_nsa = _nsa_dec_sel(proj3, page_table, idx.reshape(BS, 128), o_c, nsa_cache, win_all, l, slope_e, g_nsa, p0)
        o_fox, lfo = _fox_decode(proj3, page_table, fox_cache, cache_fox_logf, l, bf_pad, g_fox)
        o_s5, s5r, s5i = _s5(proj_s[:, C_S5U:C_S5U + gw], prm_f, state_s5_re[l].reshape(BS, ns),
                             state_s5_im[l].reshape(BS, ns), g_s5, BS, 1, 1)
        o_gdn, gdn_st = _gdn_decode(proj3, state_gdn_conv, state_gdn, l, gdn_conv[l], gdn_a_log[l],
                                    gdn_dt_bias[l], g_gdn)
        xs = _out_proj_ln(xs, [o_nsa.reshape(BS, gw), o_s5, o_gdn.reshape(BS, gw), o_fox.reshape(BS, gw)], wo_f, *ln1, BS)
        xs = _moe_dec(xs, router_w, rb, exp_w1, exp_w3, exp_w2, l, *ln2)
        st = (proj_s[:, C_NROWS:C_NROWS + gw].reshape(BS, 1, 4, hd),
              proj_s[:, C_FKV:C_FKV + 2 * gw].reshape(BS, 1, 2, nh, hd),
              lfo[:, :, S_FF:S_FF + nh],
              jnp.concatenate([win_state[:, 1:], proj_s[:, C_WIN:C_WIN + 2 * hd].reshape(BS, 1, 2, hd)], axis=1),
              gdn_st,
              jnp.concatenate([state_gdn_conv[l][:, 1:], proj_s[:, C_GQKV:C_GQKV + 3 * gw].reshape(BS, 1, 3 * gw)], axis=1),
              s5r.reshape(BS, S5_GROUPS, S5_STATE), s5i.reshape(BS, S5_GROUPS, S5_STATE))
        for a, v in zip(acc_s, st):
            a.append(v)
    nsa_rows_p, fox_kv_p, fox_logf_p, nsa_win_p, gdn_p, gdn_conv_p, s5_re_p, s5_im_p = [jnp.stack(a) for a in acc_p]
    nsa_rows_s, fox_kv_s, fox_logf_s, nsa_win_s, gdn_s, gdn_conv_s, s5_re_s, s5_im_s = [jnp.stack(a) for a in acc_s]
    return (xp.reshape(B, L, D), xs.reshape(BS, 1, D), nsa_rows_p, nsa_rows_s, fox_kv_p, fox_kv_s, fox_logf_p, fox_logf_s,
            nsa_win_p, nsa_win_s, gdn_p, gdn_s, gdn_conv_p, gdn_conv_s, s5_re_p, s5_re_s, s5_im_p, s5_im_s)
```

```python
import functools
import math

import numpy as np
import jax
import jax.numpy as jnp
from jax import lax
from jax.experimental import pallas as pl
from jax.experimental.pallas import tpu as pltpu

F32 = jnp.float32
BF16 = jnp.bfloat16
HI = lax.Precision.HIGHEST

DEPTH = 4
N_MIXERS = 4
GROUP_WIDTH = 256
HEAD_DIM = 64
N_HEADS = 4
CMP_BLOCK = 32
SEL_BLOCK = 64
SEL_TOPK = 16
WINDOW = 512
FORCE_SCORE = 1.0e4
S5_GROUP = 16
S5_GROUPS = 16
S5_STATE = 64
CONV_W = 4
GDN_CHUNK = 64
N_EXPERTS = 16
N_EXPERT_GROUPS = 4
EXPERTS_PER_GROUP = 4
D_EXPERT = 256
ALPHA = (2.0 * DEPTH) ** 0.25
LN_EPS = 1e-5
RMS_EPS = 1e-6
NEG = -1e30

VMEM_LIMIT = 56 * 1024 * 1024

C_GQKV, C_NQ, C_FKV, C_NROWS, C_S5U, C_GZ, C_FQ, C_WIN, C_SMALL = 0, 768, 1024, 1536, 1792, 2048, 2304, 2560, 2688
PROJ_COLS = 2816
S_NGATE, S_GA, S_GB, S_FF = 0, 12, 16, 20


_IN_PROJ_SEGS = ((908, 1676), (0, 256), (2196, 2708), (256, 512), (652, 908), (1684, 1940), (1940, 2196),
                 (512, 640), (640, 652), (1676, 1680), (1680, 1684), (2708, 2712))


def _in_proj_perm():
    return np.concatenate([np.arange(a, b) for a, b in _IN_PROJ_SEGS])


def _permute_w_in(w):
    used = sum(b - a for a, b in _IN_PROJ_SEGS)
    parts = [w[:, a:b] for a, b in _IN_PROJ_SEGS] + [jnp.zeros((w.shape[0], PROJ_COLS - used), w.dtype)]
    return jnp.concatenate(parts, axis=1)


def _dotw(a, w):
    if w.dtype == BF16:
        return jnp.dot(a.astype(BF16), w, preferred_element_type=F32)
    return jnp.dot(a, w, precision=HI, preferred_element_type=F32)


def _cparams(n_axes):
    return pltpu.CompilerParams(dimension_semantics=("arbitrary",) * n_axes, vmem_limit_bytes=VMEM_LIMIT)


def _resident(shape):
    nd = len(shape)
    return pl.BlockSpec(shape, lambda *_: (0,) * nd, pipeline_mode=pl.Buffered(1))


def _dot_nt(a, b):
    return lax.dot_general(a, b, (((1,), (1,)), ((), ())), preferred_element_type=F32)


def _dot_tn(a, b):
    return lax.dot_general(a, b, (((0,), (0,)), ((), ())), preferred_element_type=F32)


def _rms(x, g):
    return x * lax.rsqrt(jnp.mean(x * x, axis=-1, keepdims=True) + RMS_EPS) * g


def _ln(x, g, b):
    mu = jnp.mean(x, axis=-1, keepdims=True)
    xc = x - mu
    var = jnp.mean(xc * xc, axis=-1, keepdims=True)
    return xc * lax.rsqrt(var + LN_EPS) * g + b


def _in_proj_body(x_ref, w_ref, o_ref, *, nch):
    x = x_ref[...]
    x = x.astype(BF16) if w_ref.dtype == BF16 else x
    for j in range(0, o_ref.shape[1], nch):
        o_ref[:, j:j + nch] = _dotw(x, w_ref[:, j:j + nch])


def _in_proj(x2d, w, tm):
    m, k = x2d.shape
    n = w.shape[1]
    return pl.pallas_call(
        functools.partial(_in_proj_body, nch=256),
        grid=(m // tm,),
        in_specs=[pl.BlockSpec((tm, k), lambda i: (i, 0)), _resident((k, n))],
        out_specs=pl.BlockSpec((tm, n), lambda i: (i, 0)),
        out_shape=jax.ShapeDtypeStruct((m, n), F32),
        compiler_params=_cparams(1),
        name="in_proj",
    )(x2d, w)


def _out_proj_body(x_ref, o0_ref, o1_ref, o2_ref, o3_ref, w_ref, g_ref, b_ref, out_ref):
    gw = GROUP_WIDTH
    y = _dotw(o0_ref[...], w_ref[0:gw, :])
    y = y + _dotw(o1_ref[...], w_ref[gw:2 * gw, :])
    y = y + _dotw(o2_ref[...], w_ref[2 * gw:3 * gw, :])
    y = y + _dotw(o3_ref[...], w_ref[3 * gw:4 * gw, :])
    out_ref[...] = _ln(ALPHA * x_ref[...] + y, g_ref[...], b_ref[...])


def _out_proj_ln(x2d, mixers, w_out, g, b, tm):
    m, d = x2d.shape
    gw = GROUP_WIDTH
    row = lambda i: (i, 0)
    return pl.pallas_call(
        _out_proj_body,
        grid=(m // tm,),
        in_specs=[pl.BlockSpec((tm, d), row)] + [pl.BlockSpec((tm, gw), row)] * 4
        + [_resident((d, d)), _resident((1, d)), _resident((1, d))],
        out_specs=pl.BlockSpec((tm, d), row),
        out_shape=jax.ShapeDtypeStruct((m, d), F32),
        compiler_params=_cparams(1),
        name="out_proj_ln",
    )(x2d, *mixers, w_out, g, b)


def _moe_gate_t(lt):
    m = jnp.max(lt, axis=0, keepdims=True)
    p = jnp.exp(lt - m)
    probs = p / jnp.sum(p, axis=0, keepdims=True)
    rows = [probs[e:e + 1, :] for e in range(N_EXPERTS)]
    n = EXPERTS_PER_GROUP
    scores = []
    for g in range(N_EXPERT_GROUPS):
        r = rows[g * n:(g + 1) * n]
        best = None
        for i in range(n):
            for j in range(i + 1, n):
                s = r[i] + r[j]
                best = s if best is None else jnp.maximum(best, s)
        scores.append(best)
    grp = jnp.zeros_like(scores[0], dtype=jnp.int32)
    top = scores[0]
    for g in range(1, N_EXPERT_GROUPS):
        take = scores[g] > top
        grp = jnp.where(take, g, grp)
        top = jnp.where(take, scores[g], top)
    vals = []
    for j in range(n):
        v = rows[j]
        for g in range(1, N_EXPERT_GROUPS):
            v = jnp.where(grp == g, rows[g * n + j], v)
        vals.append(v)

    def first_argmax(vs):
        idx = jnp.zeros_like(grp)
        best = vs[0]
        for j in range(1, n):
            take = vs[j] > best
            idx = jnp.where(take, j, idx)
            best = jnp.where(take, vs[j], best)
        return best, idx

    v1, i1 = first_argmax(vals)
    v2, i2 = first_argmax([jnp.where(i1 == j, -jnp.inf, vals[j]) for j in range(n)])
    tot = v1 + v2
    w1, w2 = v1 / tot, v2 / tot
    e1, e2 = grp * n + i1, grp * n + i2
    gate = [jnp.where(e1 == e, w1, 0.0) + jnp.where(e2 == e, w2, 0.0) for e in range(N_EXPERTS)]
    return jnp.concatenate(gate, axis=0)


def _moe_body(x_ref, rw_ref, rb_ref, w1_ref, w3_ref, w2_ref, g_ref, b_ref, out_ref, *, epc):
    x = x_ref[...]
    tm = x.shape[0]
    logits = jnp.dot(x, rw_ref[...], precision=HI, preferred_element_type=F32) + rb_ref[...]
    gate = _moe_gate_t(logits.T).T
    xb = x.astype(BF16)
    cw = epc * D_EXPERT
    y = jnp.zeros((tm, x.shape[1]), F32)
    for c in range(N_EXPERTS // epc):
        a = jnp.dot(xb, w1_ref[:, c * cw:(c + 1) * cw], preferred_element_type=F32)
        b = jnp.dot(xb, w3_ref[:, c * cw:(c + 1) * cw], preferred_element_type=F32)
        ge = jnp.concatenate(
            [jnp.broadcast_to(gate[:, e:e + 1], (tm, D_EXPERT)) for e in range(c * epc, (c + 1) * epc)], axis=1)
        h = (a * jax.nn.sigmoid(a)) * b * ge
        y = y + jnp.dot(h.astype(BF16), w2_ref[c * cw:(c + 1) * cw, :], preferred_element_type=F32)
    out_ref[...] = _ln(ALPHA * x + y, g_ref[...], b_ref[...])


def _moe_ln(x2d, rw, rb, w1, w3, w2, g, b, tm):
    m, d = x2d.shape
    ne = N_EXPERTS * D_EXPERT
    row = lambda i: (i, 0)
    return pl.pallas_call(
        functools.partial(_moe_body, epc=4),
        grid=(m // tm,),
        in_specs=[pl.BlockSpec((tm, d), row), _resident((d, N_EXPERTS)), _resident((1, N_EXPERTS)),
                  _resident((d, ne)), _resident((d, ne)), _resident((ne, d)), _resident((1, d)), _resident((1, d))],
        out_specs=pl.BlockSpec((tm, d), row),
        out_shape=jax.ShapeDtypeStruct((m, d), F32),
        compiler_params=_cparams(1),
        name="moe_ln",
    )(x2d, rw, rb, w1, w3, w2, g, b)


def _moe_dec_body(x_ref, rw_ref, rb_ref, w1_ref, w3_ref, w2_ref, g_ref, b_ref, out_ref, ge_ref, acc_ref, *, epc):
    c = pl.program_id(0)
    x = x_ref[...]
    tm = x.shape[0]

    @pl.when(c == 0)
    def _():
        logits = jnp.dot(x, rw_ref[...], precision=HI, preferred_element_type=F32) + rb_ref[...]
        gate = _moe_gate_t(logits.T).T
        for cc in range(N_EXPERTS // epc):
            ge_ref[cc] = jnp.concatenate(
                [jnp.broadcast_to(gate[:, e:e + 1], (tm, D_EXPERT)) for e in range(cc * epc, (cc + 1) * epc)], axis=1)
        acc_ref[...] = jnp.zeros(acc_ref.shape, F32)

    ge = ge_ref[c]
    y = acc_ref[...]
    for e in range(epc):
        a = _dotw(x, w1_ref[0, e])
        b = _dotw(x, w3_ref[0, e])
        h = (a * jax.nn.sigmoid(a)) * b * ge[:, e * D_EXPERT:(e + 1) * D_EXPERT]
        y = y + _dotw(h, w2_ref[0, e])
    acc_ref[...] = y

    @pl.when(c == pl.num_programs(0) - 1)
    def _():
        out_ref[...] = _ln(ALPHA * x + acc_ref[...], g_ref[...], b_ref[...])


def _moe_dec(x2d, rw, rb, w1, w3, w2, layer, g, b, epc=2):
    m, d = x2d.shape
    cw = epc * D_EXPERT
    nchunk = N_EXPERTS // epc
    const = lambda c: (0, 0)
    return pl.pallas_call(
        functools.partial(_moe_dec_body, epc=epc),
        grid=(nchunk,),
        in_specs=[pl.BlockSpec((m, d), const), pl.BlockSpec((d, N_EXPERTS), const), pl.BlockSpec((1, N_EXPERTS), const),
                  pl.BlockSpec((1, epc, d, D_EXPERT), lambda c: (layer, c, 0, 0)),
                  pl.BlockSpec((1, epc, d, D_EXPERT), lambda c: (layer, c, 0, 0)),
                  pl.BlockSpec((1, epc, D_EXPERT, d), lambda c: (layer, c, 0, 0)), pl.BlockSpec((1, d), const),
                  pl.BlockSpec((1, d), const)],
        out_specs=pl.BlockSpec((m, d), const),
        out_shape=jax.ShapeDtypeStruct((m, d), F32),
        scratch_shapes=[pltpu.VMEM((nchunk, m, cw), F32), pltpu.VMEM((m, d), F32)],
        compiler_params=_cparams(1),
        name="moe_dec",
    )(x2d, rw, rb, w1, w3, w2, g, b)


def _fox_prep_body(ff_ref, bf_ref, logf_ref, cum_ref):
    lf = jax.nn.log_sigmoid(ff_ref[0] + bf_ref[...])
    logf_ref[0] = lf
    n = lf.shape[1]
    lane = lax.broadcasted_iota(jnp.int32, lf.shape, 1)
    c = lf
    s = 1
    while s < n:
        c = c + jnp.where(lane >= s, pltpu.roll(c, s, 1), 0.0)
        s *= 2
    cum_ref[0] = c


def _fox_prep(ff_t, bf):
    bsz, nh, n = ff_t.shape
    blk = pl.BlockSpec((1, nh, n), lambda b: (b, 0, 0))
    return pl.pallas_call(
        _fox_prep_body,
        grid=(bsz,),
        in_specs=[blk, _resident((nh, 1))],
        out_specs=[blk, blk],
        out_shape=[jax.ShapeDtypeStruct((bsz, nh, n), F32)] * 2,
        compiler_params=_cparams(1),
        name="fox_prep",
    )(ff_t, bf)


def _fox_body(q_ref, kv_ref, cq_ref, ck_ref, g_ref, o_ref, m_ref, l_ref, acc_ref, *, tq):
    qi = pl.program_id(1)
    hd, nh, gw = HEAD_DIM, N_HEADS, GROUP_WIDTH
    scale = hd ** -0.5
    q = q_ref[...] * scale
    qb = [q[:, h * hd:(h + 1) * hd].astype(BF16) for h in range(nh)]
    cq = cq_ref[0]
    m_ref[...] = jnp.full(m_ref.shape, NEG, F32)
    l_ref[...] = jnp.zeros(l_ref.shape, F32)
    acc_ref[...] = jnp.zeros(acc_ref.shape, F32)
    row = lax.broadcasted_iota(jnp.int32, (tq, tq), 0)
    col = lax.broadcasted_iota(jnp.int32, (tq, tq), 1)

    def chunk(kc, diagonal):
        r0 = pl.multiple_of(kc * tq, tq)
        kv = kv_ref[pl.ds(r0, tq), :]
        ck = ck_ref[0, :, pl.ds(r0, tq)]
        for h in range(nh):
            k = kv[:, h * hd:(h + 1) * hd].astype(BF16)
            v = kv[:, gw + h * hd:gw + (h + 1) * hd].astype(BF16)
            s = _dot_nt(qb[h], k) + cq[:, h:h + 1] - ck[h:h + 1, :]
            if diagonal:
                s = jnp.where(col <= row, s, NEG)
            m_old = m_ref[h]
            m_new = jnp.maximum(m_old, jnp.max(s, axis=1, keepdims=True))
            a = jnp.exp(m_old - m_new)
            p = jnp.exp(s - m_new)
            l_ref[h] = a * l_ref[h] + jnp.sum(p, axis=1, keepdims=True)
            acc_ref[:, h * hd:(h + 1) * hd] = (a * acc_ref[:, h * hd:(h + 1) * hd]
                                               + jnp.dot(p.astype(BF16), v, preferred_element_type=F32))
            m_ref[h] = m_new

    def body(kc, carry):
        chunk(kc, False)
        return carry

    lax.fori_loop(0, qi, body, 0)
    chunk(qi, True)
    o = jnp.concatenate([acc_ref[:, h * hd:(h + 1) * hd] / jnp.maximum(l_ref[h], 1e-30) for h in range(nh)], axis=1)
    o_ref[...] = _rms(o, g_ref[...])


def _fox_prompt(proj, cq, ck, g, bsz, n, tq):
    nh, gw = N_HEADS, GROUP_WIDTH
    nq = n // tq
    return pl.pallas_call(
        functools.partial(_fox_body, tq=tq),
        grid=(bsz, nq),
        in_specs=[pl.BlockSpec((tq, gw), lambda b, i: (b * nq + i, C_FQ // gw)),
                  pl.BlockSpec((n, 2 * gw), lambda b, i: (b, C_FKV // (2 * gw))),
                  pl.BlockSpec((1, tq, nh), lambda b, i: (b, i, 0)),
                  pl.BlockSpec((1, nh, n), lambda b, i: (b, 0, 0)),
                  _resident((1, gw))],
        out_specs=pl.BlockSpec((tq, gw), lambda b, i: (b * nq + i, 0)),
        out_shape=jax.ShapeDtypeStruct((bsz * n, gw), F32),
        scratch_shapes=[pltpu.VMEM((nh, tq, 1), F32), pltpu.VMEM((nh, tq, 1), F32), pltpu.VMEM((tq, gw), F32)],
        compiler_params=_cparams(2),
        name="fox_prompt",
    )(proj, proj, cq, ck, g)


def _slope_col(slopes_ref, tq):
    hrow = lax.broadcasted_iota(jnp.int32, (N_HEADS * tq, 1), 0) // tq
    s = jnp.full((N_HEADS * tq, 1), slopes_ref[0], F32)
    for h in range(1, N_HEADS):
        s = jnp.where(hrow == h, slopes_ref[h], s)
    return s


def _softmax_rows(s, mask):
    s = jnp.where(mask, s, NEG)
    m = jnp.max(s, axis=1, keepdims=True)
    p = jnp.where(mask, jnp.exp(s - m), 0.0)
    return p / jnp.maximum(jnp.sum(p, axis=1, keepdims=True), 1e-30)


def _nsa_body(slopes_ref, q_ref, rows_ref, win_ref, small_ref, pw_ref, pair_ref, exp_ref, g_ref, o_ref,
              kvc_ref, m_ref, l_ref, acc_ref, *, tq, tk, n):
    qi = pl.program_id(1)
    hd, nh = HEAD_DIM, N_HEADS
    scale = hd ** -0.5
    n_cmp, n_sel = n // CMP_BLOCK, n // SEL_BLOCK
    r4 = nh * tq

    @pl.when(qi == 0)
    def _():
        kv = rows_ref[:, 0:2 * hd].reshape(n_cmp, CMP_BLOCK, 2 * hd)
        kvc_ref[...] = jnp.sum(kv * pw_ref[...][None], axis=1)

    q = q_ref[...] * scale
    qs = jnp.concatenate([q[:, h * hd:(h + 1) * hd] for h in range(nh)], axis=0).astype(BF16)
    slope = _slope_col(slopes_ref, tq)
    q0 = qi * tq
    t_row = q0 + lax.broadcasted_iota(jnp.int32, (r4, 1), 0) % tq
    t_q = q0 + lax.broadcasted_iota(jnp.int32, (tq, 1), 0)

    kvc = kvc_ref[...]
    s_c = _dot_nt(qs, kvc[:, 0:hd].astype(BF16))
    cmp_end = (lax.broadcasted_iota(jnp.int32, (1, n_cmp), 1) + 1) * CMP_BLOCK - 1
    dist_c = t_row - cmp_end
    p_c = _softmax_rows(s_c - slope * dist_c.astype(F32), dist_c >= 0)
    o_c = jnp.dot(p_c.astype(BF16), kvc[:, hd:2 * hd].astype(BF16), preferred_element_type=F32)

    psum = p_c[0:tq]
    for h in range(1, nh):
        psum = psum + p_c[h * tq:(h + 1) * tq]
    imp = lax.dot_general(pair_ref[...], psum, (((1,), (1,)), ((), ())), precision=HI,
                          preferred_element_type=F32)
    blk = lax.broadcasted_iota(jnp.int32, (n_sel, tq), 0)
    cur = (q0 + lax.broadcasted_iota(jnp.int32, (1, tq), 1)) // SEL_BLOCK
    valid = blk <= cur
    forced = (blk == cur) | (blk == 0)
    score = jnp.where(valid, jnp.where(forced, FORCE_SCORE, imp), -jnp.inf)
    rank = jnp.zeros((n_sel, tq), jnp.int32)
    for j in range(n_sel):
        sj = score[j:j + 1, :]
        beats = (sj > score) | ((sj == score) & (blk > j))
        rank = rank + beats.astype(jnp.int32)
    sel = ((rank < SEL_TOPK) & valid).astype(BF16)

    far = 1e9

    m_ref[...] = jnp.full(m_ref.shape, NEG, F32)
    l_ref[...] = jnp.zeros(l_ref.shape, F32)
    acc_ref[...] = jnp.zeros(acc_ref.shape, F32)
    lane_k = lax.broadcasted_iota(jnp.int32, (1, tk), 1)

    def body(kc, carry):
        r0 = pl.multiple_of(kc * tk, tk)
        kvs = rows_ref[pl.ds(r0, tk), 2 * hd:4 * hd]
        ks, vs = kvs[:, 0:hd].astype(BF16), kvs[:, hd:2 * hd].astype(BF16)
        dist = t_q - (r0 + lane_k)
        keep = _dot_tn(sel, exp_ref[kc])
        dmask = jnp.where((dist >= 0) & (keep > 0.5), dist.astype(F32), far)
        s = _dot_nt(qs, ks) - slope * jnp.concatenate([dmask] * nh, axis=0)
        m_old = m_ref[...]
        m_new = jnp.maximum(m_old, jnp.max(s, axis=1, keepdims=True))
        a = jnp.exp(m_old - m_new)
        p = jnp.exp(s - m_new)
        l_ref[...] = a * l_ref[...] + jnp.sum(p, axis=1, keepdims=True)
        acc_ref[...] = a * acc_ref[...] + jnp.dot(p.astype(BF16), vs, preferred_element_type=F32)
        m_ref[...] = m_new
        return carry

    lax.fori_loop(0, (q0 + tq + tk - 1) // tk, body, 0)
    o_s = acc_ref[...] / l_ref[...]

    wlen = WINDOW + tq
    w0 = pl.multiple_of(jnp.maximum(q0 - WINDOW, 0), tq)
    wkv = win_ref[pl.ds(w0, wlen), :]
    dist_w = t_q - (w0 + lax.broadcasted_iota(jnp.int32, (1, wlen), 1))
    dmask_w = jnp.where((dist_w >= 0) & (dist_w < WINDOW), dist_w.astype(F32), far)
    s_w = _dot_nt(qs, wkv[:, 0:hd].astype(BF16)) - slope * jnp.concatenate([dmask_w] * nh, axis=0)
    p_w = jnp.exp(s_w - jnp.max(s_w, axis=1, keepdims=True))
    o_w = (jnp.dot(p_w.astype(BF16), wkv[:, hd:2 * hd].astype(BF16), preferred_element_type=F32)
           / jnp.sum(p_w, axis=1, keepdims=True))

    gt = jax.nn.sigmoid(small_ref[:, S_NGATE:S_NGATE + 3 * nh])
    outs = []
    for h in range(nh):
        sl = slice(h * tq, (h + 1) * tq)
        outs.append(gt[:, 3 * h:3 * h + 1] * o_c[sl] + gt[:, 3 * h + 1:3 * h + 2] * o_s[sl]
                    + gt[:, 3 * h + 2:3 * h + 3] * o_w[sl])
    o_ref[...] = _rms(jnp.concatenate(outs, axis=1), g_ref[...])


def _nsa_prompt(proj, slopes, pool_w, g, bsz, n, tq=128, tk=512):
    hd, nh, gw = HEAD_DIM, N_HEADS, GROUP_WIDTH
    assert n % tk == 0 and n >= WINDOW + tq and n % SEL_BLOCK == 0
    nq = n // tq
    n_cmp, n_sel = n // CMP_BLOCK, n // SEL_BLOCK
    pw = jnp.concatenate([jnp.broadcast_to(pool_w[0][:, None], (CMP_BLOCK, hd)),
                          jnp.broadcast_to(pool_w[1][:, None], (CMP_BLOCK, hd))], axis=1)
    pair = (np.arange(n_sel)[:, None] == np.arange(n_cmp)[None, :] // 2).astype(np.float32)
    expand = (np.arange(n_sel)[:, None] == np.arange(n)[None, :] // SEL_BLOCK)
    expand = jnp.asarray(expand.reshape(n_sel, n // tk, tk).transpose(1, 0, 2), BF16)
    return pl.pallas_call(
        functools.partial(_nsa_body, tq=tq, tk=tk, n=n),
        grid=(bsz, nq),
        in_specs=[pl.BlockSpec(memory_space=pltpu.SMEM),
                  pl.BlockSpec((tq, gw), lambda b, i: (b * nq + i, C_NQ // gw)),
                  pl.BlockSpec((n, gw), lambda b, i: (b, C_NROWS // gw)),
                  pl.BlockSpec((n, 2 * hd), lambda b, i: (b, C_WIN // (2 * hd))),
                  pl.BlockSpec((tq, 128), lambda b, i: (b * nq + i, C_SMALL // 128)),
                  _resident((CMP_BLOCK, 2 * hd)), _resident((n_sel, n_cmp)), _resident((n // tk, n_sel, tk)),
                  _resident((1, gw))],
        out_specs=pl.BlockSpec((tq, gw), lambda b, i: (b * nq + i, 0)),
        out_shape=jax.ShapeDtypeStruct((bsz * n, gw), F32),
        scratch_shapes=[pltpu.VMEM((n_cmp, 2 * hd), F32), pltpu.VMEM((nh * tq, 1), F32),
                        pltpu.VMEM((nh * tq, 1), F32), pltpu.VMEM((nh * tq, hd), F32)],
        compiler_params=_cparams(2),
        name="nsa_prompt",
    )(slopes, proj, proj, proj, proj, pw, jnp.asarray(pair), expand, g)


def _s5_body(u_ref, bre_ref, bim_ref, are_ref, aim_ref, s0r_ref, s0i_ref, cre_ref, cim_ref, d_ref, wg_ref, g_ref,
             o_ref, fr_ref, fi_ref, xr_ref, xi_ref, sr_ref, si_ref, *, bsz, tc):
    @pl.when(pl.program_id(0) == 0)
    def _():
        sr_ref[...] = s0r_ref[...]
        si_ref[...] = s0i_ref[...]

    u = u_ref[...]
    xr_ref[...] = _dotw(u, bre_ref[...])
    xi_ref[...] = _dotw(u, bim_ref[...])
    ar = jnp.broadcast_to(are_ref[...], sr_ref.shape)
    ai = jnp.broadcast_to(aim_ref[...], sr_ref.shape)

    def step(t, carry):
        sr, si = carry
        r0 = pl.multiple_of(t * bsz, bsz)
        nr = ar * sr - ai * si + xr_ref[pl.ds(r0, bsz), :]
        ni = ar * si + ai * sr + xi_ref[pl.ds(r0, bsz), :]
        xr_ref[pl.ds(r0, bsz), :] = nr
        xi_ref[pl.ds(r0, bsz), :] = ni
        return nr, ni

    sr, si = lax.fori_loop(0, tc, step, (sr_ref[...], si_ref[...]))
    sr_ref[...] = sr
    si_ref[...] = si
    fr_ref[...] = sr
    fi_ref[...] = si
    y = _dotw(xr_ref[...], cre_ref[...]) - _dotw(xi_ref[...], cim_ref[...])
    y = jax.nn.gelu(y + d_ref[...] * u)
    out = y * jax.nn.sigmoid(_dotw(y, wg_ref[...]))
    o_ref[...] = _rms(out, g_ref[...])


def _s5(u_tm, prm, s0r, s0i, g, bsz, n, tc):
    gw = GROUP_WIDTH
    ns = S5_GROUPS * S5_STATE
    rows = tc * bsz
    st = jax.ShapeDtypeStruct((bsz, ns), F32)
    return pl.pallas_call(
        functools.partial(_s5_body, bsz=bsz, tc=tc),
        grid=(n // tc,),
        in_specs=[pl.BlockSpec((rows, gw), lambda i: (i, 0)),
                  _resident((gw, ns)), _resident((gw, ns)), _resident((1, ns)), _resident((1, ns)),
                  _resident((bsz, ns)), _resident((bsz, ns)), _resident((ns, gw)), _resident((ns, gw)),
                  _resident((1, gw)), _resident((gw, gw)), _resident((1, gw))],
        out_specs=[pl.BlockSpec((rows, gw), lambda i: (i, 0)), pl.BlockSpec((bsz, ns), lambda i: (0, 0)),
                   pl.BlockSpec((bsz, ns), lambda i: (0, 0))],
        out_shape=[jax.ShapeDtypeStruct((n * bsz, gw), F32), st, st],
        scratch_shapes=[pltpu.VMEM((rows, ns), F32), pltpu.VMEM((rows, ns), F32),
                        pltpu.VMEM((bsz, ns), F32), pltpu.VMEM((bsz, ns), F32)],
        compiler_params=_cparams(1),
        name="s5",
    )(u_tm, prm["bre"], prm["bim"], prm["are"], prm["aim"], s0r, s0i, prm["cre"], prm["cim"], prm["d"], prm["wg"], g)


def _s5_params(a_re, a_im, b_re, b_im, c_re, c_im, d, log_dt, w_glu, wdt):
    dt = jnp.exp(log_dt)[:, None]
    mag = jnp.exp(dt * a_re)
    abar_re, abar_im = mag * jnp.cos(dt * a_im), mag * jnp.sin(dt * a_im)
    den = a_re * a_re + a_im * a_im
    num_re, num_im = abar_re - 1.0, abar_im
    zoh_re = (num_re * a_re + num_im * a_im) / den
    zoh_im = (num_im * a_re - num_re * a_im) / den
    bbar_re = zoh_re[..., None] * b_re - zoh_im[..., None] * b_im
    bbar_im = zoh_re[..., None] * b_im + zoh_im[..., None] * b_re
    eye = jnp.eye(S5_GROUPS, dtype=F32)
    ns = S5_GROUPS * S5_STATE
    to_in = lambda w: (w[:, :, None, :] * eye[:, None, :, None]).transpose(0, 3, 2, 1).reshape(GROUP_WIDTH, ns).astype(wdt)
    to_out = lambda w: (w[:, :, None, :] * eye[:, None, :, None]).transpose(0, 3, 2, 1).reshape(ns, GROUP_WIDTH).astype(wdt)
    return {"bre": to_in(bbar_re), "bim": to_in(bbar_im), "are": abar_re.reshape(1, ns), "aim": abar_im.reshape(1, ns),
            "cre": to_out(c_re), "cim": to_out(c_im), "d": d.reshape(1, GROUP_WIDTH), "wg": w_glu.astype(wdt)}


def _mm_bf16(a, b):
    return jnp.dot(a.astype(BF16), b.astype(BF16), preferred_element_type=F32)


def _l2n(x):
    return x * lax.rsqrt(jnp.sum(x * x, axis=-1, keepdims=True) + RMS_EPS)


def _gdn_body(qkv_ref, z_ref, small_ref, at_ref, cb_ref, s0_ref, cw_ref, alr_ref, dtr_ref, alc_ref, dtc_ref, ng_ref,
              o_ref, sfin_ref, ext_ref, s_ref, *, tc):
    j = pl.program_id(1)
    hd, nh, gw, ch = HEAD_DIM, N_HEADS, GROUP_WIDTH, GDN_CHUNK
    pad = 8

    @pl.when(j == 0)
    def _():
        ext_ref[0:pad, :] = cb_ref[0]
        s_ref[...] = s0_ref[0]

    @pl.when(j > 0)
    def _():
        ext_ref[0:pad, :] = ext_ref[tc:tc + pad, :]

    ext_ref[pad:tc + pad, :] = qkv_ref[...]
    base = pad - (CONV_W - 1)
    conv = ext_ref[base:base + tc, :] * cw_ref[0:1, :]
    for t in range(1, CONV_W):
        conv = conv + ext_ref[base + t:base + t + tc, :] * cw_ref[t:t + 1, :]
    conv = conv * jax.nn.sigmoid(conv)

    small = small_ref[...]
    g_col = -jnp.exp(alr_ref[...]) * jax.nn.softplus(small[:, S_GA:S_GA + nh] + dtr_ref[...])
    beta = jax.nn.sigmoid(small[:, S_GB:S_GB + nh])
    g_row = -jnp.exp(alc_ref[...]) * jax.nn.softplus(at_ref[0][0:nh, :] + dtc_ref[...])
    ri = lax.broadcasted_iota(jnp.int32, (tc, tc), 0)
    ci = lax.broadcasted_iota(jnp.int32, (tc, tc), 1)
    same = (ri // ch) == (ci // ch)
    tril = same & (ci <= ri)
    strict = same & (ci < ri)
    tril_f = tril.astype(F32)
    cum_col = jnp.dot(tril_f, g_col, precision=HI, preferred_element_type=F32)
    cum_row = lax.dot_general(g_row, tril_f, (((1,), (1,)), ((), ())), precision=HI,
                              preferred_element_type=F32)
    z = z_ref[...]
    outs = []
    for h in range(nh):
        gc = cum_col[:, h:h + 1]
        decay = jnp.where(tril, jnp.exp(jnp.where(tril, gc - cum_row[h:h + 1, :], 0.0)), 0.0)
        q = _l2n(conv[:, h * hd:(h + 1) * hd]) * hd ** -0.5
        k = _l2n(conv[:, gw + h * hd:gw + (h + 1) * hd])
        v = conv[:, 2 * gw + h * hd:2 * gw + (h + 1) * hd]
        bc = beta[:, h:h + 1]
        kb = k * bc
        kbb, kbf = kb.astype(BF16), k.astype(BF16)
        m = jnp.where(strict, _dot_nt(kbb, kbf) * decay, 0.0)
        nmat = -m
        qm = m
        for _ in range(int(math.log2(ch)) - 1):
            qm = _mm_bf16(qm, qm)
            nmat = nmat + qm + _mm_bf16(nmat, qm)
        eg = jnp.exp(gc)
        rhs = jnp.concatenate([v * bc, kb * eg], axis=1)
        uw = rhs + _mm_bf16(nmat, rhs)
        u, w = uw[:, 0:hd], uw[:, hd:2 * hd]
        attn = (_dot_nt(q.astype(BF16), kbf) * decay).astype(BF16)
        qg = (q * eg).astype(BF16)
        s = s_ref[h]
        o_chunks = []
        for c in range(tc // ch):
            r = slice(c * ch, (c + 1) * ch)
            sb = s.astype(BF16)
            v_new = u[r] - jnp.dot(w[r].astype(BF16), sb, preferred_element_type=F32)
            vb = v_new.astype(BF16)
            o_chunks.append(jnp.dot(qg[r], sb, preferred_element_type=F32)
                            + jnp.dot(attn[r, r], vb, preferred_element_type=F32))
            g_last = gc[(c + 1) * ch - 1:(c + 1) * ch, :]
            kd = (k[r] * jnp.exp(g_last - gc[r])).astype(BF16)
            s = s * jnp.exp(g_last) + _dot_tn(kd, vb)
        s_ref[h] = s
        o = jnp.concatenate(o_chunks, axis=0)
        o = o * lax.rsqrt(jnp.mean(o * o, axis=-1, keepdims=True) + RMS_EPS) * ng_ref[:, h * hd:(h + 1) * hd]
        zh = z[:, h * hd:(h + 1) * hd]
        outs.append(o * (zh * jax.nn.sigmoid(zh)))
    o_ref[...] = jnp.concatenate(outs, axis=1)
    sfin_ref[0] = s_ref[...]


def _gdn_prompt(proj, a_t, conv_buf8, s0, conv_w, a_log, dt_bias, norm_g, bsz, n, tc=256):
    hd, nh, gw = HEAD_DIM, N_HEADS, GROUP_WIDTH
    nb = n // tc
    return pl.pallas_call(
        functools.partial(_gdn_body, tc=tc),
        grid=(bsz, nb),
        in_specs=[pl.BlockSpec((tc, 3 * gw), lambda b, j: (b * nb + j, C_GQKV // (3 * gw))),
                  pl.BlockSpec((tc, gw), lambda b, j: (b * nb + j, C_GZ // gw)),
                  pl.BlockSpec((tc, 128), lambda b, j: (b * nb + j, C_SMALL // 128)),
                  pl.BlockSpec((1, 8, tc), lambda b, j: (b, 0, j)),
                  pl.BlockSpec((1, 8, 3 * gw), lambda b, j: (b, 0, 0)),
                  pl.BlockSpec((1, nh, hd, hd), lambda b, j: (b, 0, 0, 0)),
                  _resident((CONV_W, 3 * gw)), _resident((1, nh)), _resident((1, nh)), _resident((nh, 1)),
                  _resident((nh, 1)), _resident((1, gw))],
        out_specs=[pl.BlockSpec((tc, gw), lambda b, j: (b * nb + j, 0)),
                   pl.BlockSpec((1, nh, hd, hd), lambda b, j: (b, 0, 0, 0))],
        out_shape=[jax.ShapeDtypeStruct((bsz * n, gw), F32), jax.ShapeDtypeStruct((bsz, nh, hd, hd), F32)],
        scratch_shapes=[pltpu.VMEM((tc + 8, 3 * gw), F32), pltpu.VMEM((nh, hd, hd), F32)],
        compiler_params=_cparams(2),
        name="gdn_prompt",
    )(proj, proj, proj, a_t, conv_buf8, s0, conv_w, a_log.reshape(1, nh), dt_bias.reshape(1, nh),
      a_log.reshape(nh, 1), dt_bias.reshape(nh, 1), norm_g)


def _head_ones():
    h = np.arange(GROUP_WIDTH) // HEAD_DIM
    return jnp.asarray((h[:, None] == h[None, :]).astype(np.float32))


def _head_scores(k4, q, ones):
    return jnp.dot(k4 * q, ones, precision=HI, preferred_element_type=F32) * HEAD_DIM ** -0.5


def _tile4(x):
    return jnp.concatenate([x] * N_HEADS, axis=1)


def _fox_dec_body(pt_ref, q_ref, kvn_ref, ffn_ref, bf_ref, g_ref, *rest, npg, nstep):
    kv_refs, lf_refs = rest[0:npg], rest[npg:2 * npg]
    o_ref, lfo_ref, m_ref, l_ref, acc_ref, car_ref = rest[2 * npg:]
    j = pl.program_id(1)
    gw, hd, nh = GROUP_WIDTH, HEAD_DIM, N_HEADS
    scale = hd ** -0.5
    q = q_ref[0]
    npos = kv_refs[0].shape[3]

    def per_head(x):
        return jnp.sum(x.reshape(nh, hd, x.shape[1]), axis=1)

    def spread(x):
        return jnp.broadcast_to(x[:, None, :], (nh, hd, x.shape[1])).reshape(gw, x.shape[1])

    @pl.when(j == 0)
    def _():
        lf_new = jax.nn.log_sigmoid(ffn_ref[0] + bf_ref[...])
        lfo_ref[0] = lf_new
        m_ref[...] = jnp.full(m_ref.shape, NEG, F32)
        l_ref[...] = jnp.zeros(l_ref.shape, F32)
        acc_ref[...] = jnp.zeros(acc_ref.shape, F32)
        car_ref[...] = jnp.broadcast_to(lf_new, car_ref.shape)

    ri = lax.broadcasted_iota(jnp.int32, (npos, npos), 0)
    ci = lax.broadcasted_iota(jnp.int32, (npos, npos), 1)
    later = (ri > ci).astype(F32)
    m, l, acc, car = m_ref[...], l_ref[...], acc_ref[...], car_ref[...]
    for i in range(npg):
        kv = kv_refs[i][0, 0]
        lf = lf_refs[i][0, 0]
        suf = jnp.dot(lf, later, precision=HI, preferred_element_type=F32)
        s = per_head(kv[0:gw, :] * q) * scale + car + suf
        m_new = jnp.maximum(m, s)
        a = jnp.exp(m - m_new)
        p = jnp.exp(s - m_new)
        l = a * l + p
        acc = spread(a) * acc + spread(p) * kv[gw:2 * gw, :]
        m = m_new
        car = car + jnp.sum(lf, axis=1, keepdims=True)
    m_ref[...], l_ref[...], acc_ref[...], car_ref[...] = m, l, acc, car

    @pl.when(j == nstep - 1)
    def _():
        kvn = kvn_ref[0]
        s_new = per_head(kvn[0:gw, :] * q) * scale
        m_tot = jnp.maximum(jnp.max(m, axis=1, keepdims=True), s_new)
        w = jnp.exp(m - m_tot)
        p_new = jnp.exp(s_new - m_tot)
        l_tot = jnp.sum(l * w, axis=1, keepdims=True) + p_new
        acc_tot = jnp.sum(acc * spread(w), axis=1, keepdims=True) + spread(p_new) * kvn[gw:2 * gw, :]
        o = acc_tot / spread(l_tot)
        o_ref[0] = o * lax.rsqrt(jnp.mean(o * o, axis=0, keepdims=True) + RMS_EPS) * g_ref[...]


def _fox_decode(q_col, kvn_col, ffn_col, page_table, kv_cache_t, lf_cache_t, layer, bf_col, g_col, npg=8):
    bsz = q_col.shape[0]
    n_pages = page_table.shape[1]
    gw, nh = GROUP_WIDTH, N_HEADS
    npos = kv_cache_t.shape[3]
    nstep = n_pages // npg
    page = lambda i: (lambda b, j, pt: (layer, pt[b, n_pages - 1 - (j * npg + i)], 0, 0))
    const2 = lambda b, j, pt: (0, 0)
    per_b = lambda b, j, pt: (b, 0, 0)
    grid_spec = pltpu.PrefetchScalarGridSpec(
        num_scalar_prefetch=1,
        grid=(bsz, nstep),
        in_specs=[pl.BlockSpec((1, gw, 1), per_b), pl.BlockSpec((1, 2 * gw, 1), per_b), pl.BlockSpec((1, nh, 1), per_b),
                  pl.BlockSpec((nh, 1), const2), pl.BlockSpec((gw, 1), const2)]
        + [pl.BlockSpec((1, 1, 2 * gw, npos), page(i)) for i in range(npg)]
        + [pl.BlockSpec((1, 1, nh, npos), page(i)) for i in range(npg)],
        out_specs=[pl.BlockSpec((1, gw, 1), per_b), pl.BlockSpec((1, nh, 1), per_b)],
        scratch_shapes=[pltpu.VMEM((nh, npos), F32), pltpu.VMEM((nh, npos), F32), pltpu.VMEM((gw, npos), F32),
                        pltpu.VMEM((nh, npos), F32)],
    )
    return pl.pallas_call(
        functools.partial(_fox_dec_body, npg=npg, nstep=nstep),
        grid_spec=grid_spec,
        out_shape=[jax.ShapeDtypeStruct((bsz, gw, 1), F32), jax.ShapeDtypeStruct((bsz, nh, 1), F32)],
        compiler_params=_cparams(2),
        name="fox_decode",
    )(page_table, q_col, kvn_col, ffn_col, bf_col, g_col, *([kv_cache_t] * npg), *([lf_cache_t] * npg))


def _nsa_dec_cmp_body(pt_ref, q_ref, pw_ref, slope_ref, ones_ref, *rest, npg, nstep, p0):
    pg_refs = rest[0:npg]
    oc_ref, idx_ref, kvc_ref, imp_ref = rest[npg:]
    j = pl.program_id(1)
    gw, hd, nh = GROUP_WIDTH, HEAD_DIM, N_HEADS
    per_page = 128 // CMP_BLOCK
    pw = pw_ref[...]
    pooled = [jnp.sum((pg_refs[i][0, 0].T * pw).reshape(per_page, CMP_BLOCK, 2 * hd), axis=1) for i in range(npg)]
    r0 = pl.multiple_of(j * (npg * per_page), npg * per_page)
    kvc_ref[pl.ds(r0, npg * per_page), :] = jnp.concatenate(pooled, axis=0)

    @pl.when(j == nstep - 1)
    def _():
        n_cmp = kvc_ref.shape[0]
        n_sel = n_cmp // 2
        q = q_ref[0]
        kvc = kvc_ref[...]
        cmp_end = (lax.broadcasted_iota(jnp.int32, (n_cmp, 1), 0) + 1) * CMP_BLOCK - 1
        dist = p0 - cmp_end
        s = _head_scores(_tile4(kvc[:, 0:hd]), q, ones_ref[...]) - slope_ref[...] * dist.astype(F32)
        ok = dist >= 0
        s = jnp.where(ok, s, NEG)
        p = jnp.where(ok, jnp.exp(s - jnp.max(s, axis=0, keepdims=True)), 0.0)
        p = p / jnp.maximum(jnp.sum(p, axis=0, keepdims=True), 1e-30)
        oc_ref[0] = jnp.sum(p * _tile4(kvc[:, hd:2 * hd]), axis=0, keepdims=True)
        imp = p[:, 0:1]
        for h in range(1, nh):
            imp = imp + p[:, h * hd:h * hd + 1]
        imp_ref[...] = jnp.broadcast_to(imp, imp_ref.shape)
        colm = imp_ref[pl.ds(0, n_sel, stride=2), :] + imp_ref[pl.ds(1, n_sel, stride=2), :]
        ri = lax.broadcasted_iota(jnp.int32, (n_sel, n_sel), 0)
        ci = lax.broadcasted_iota(jnp.int32, (n_sel, n_sel), 1)
        colm = jnp.where(ri == 0, FORCE_SCORE, colm)
        rowm = colm.T
        beats = (rowm > colm) | ((rowm == colm) & (ci < ri))
        rank = jnp.sum(beats.astype(F32), axis=1, keepdims=True)
        sel = (rank < SEL_TOPK - 1).astype(F32)
        pos = jnp.dot((ci < ri).astype(F32), jnp.broadcast_to(sel, (n_sel, n_sel)), preferred_element_type=F32)
        onehot = jnp.where((sel > 0.5) & (pos == ci.astype(F32)), ri.astype(F32), 0.0)
        idx_ref[0] = jnp.sum(onehot, axis=0, keepdims=True).astype(jnp.int32)


def _nsa_dec_cmp(proj3, page_table, cache2, layer, pw128, slope_e, p0, npg=16):
    bsz = proj3.shape[0]
    n_pages = page_table.shape[1]
    gw, hd = GROUP_WIDTH, HEAD_DIM
    nstep = n_pages // npg
    n_cmp = p0 // CMP_BLOCK
    assert p0 % SEL_BLOCK == 0 and n_cmp // 2 == 128 and p0 == n_pages * 128
    page = lambda i: (lambda b, j, pt: (layer, pt[b, j * npg + i], 0, 0))
    const2 = lambda b, j, pt: (0, 0)
    grid_spec = pltpu.PrefetchScalarGridSpec(
        num_scalar_prefetch=1,
        grid=(bsz, nstep),
        in_specs=[pl.BlockSpec((1, 1, gw), lambda b, j, pt: (b, 0, C_NQ // gw)),
                  pl.BlockSpec((128, 2 * hd), const2), pl.BlockSpec((1, gw), const2), pl.BlockSpec((gw, gw), const2)]
        + [pl.BlockSpec((1, 1, 2 * hd, cache2.shape[3]), page(i)) for i in range(npg)],
        out_specs=[pl.BlockSpec((1, 1, gw), lambda b, j, pt: (b, 0, 0)),
                   pl.BlockSpec((1, 1, 128), lambda b, j, pt: (b, 0, 0))],
        scratch_shapes=[pltpu.VMEM((n_cmp, 2 * hd), F32), pltpu.VMEM((n_cmp, 128), F32)],
    )
    return pl.pallas_call(
        functools.partial(_nsa_dec_cmp_body, npg=npg, nstep=nstep, p0=p0),
        grid_spec=grid_spec,
        out_shape=[jax.ShapeDtypeStruct((bsz, 1, gw), F32), jax.ShapeDtypeStruct((bsz, 1, 128), jnp.int32)],
        compiler_params=_cparams(2),
        name="nsa_dec_cmp",
    )(page_table, proj3, pw128, slope_e, _head_ones(), *([cache2] * npg))


def _nsa_dec_sel_body(pt_ref, idx_ref, q_ref, new_ref, wnew_ref, small_ref, oc_ref, win_ref, slope_ref, ones_ref,
                      eg_ref, g_ref, *rest, nsel, p0):
    blk_refs = rest[0:nsel]
    o_ref = rest[nsel]
    b = pl.program_id(0)
    hd = HEAD_DIM
    q = q_ref[0]
    ones = ones_ref[...]
    slope = slope_ref[...]
    new = new_ref[0]
    s_new = _head_scores(_tile4(new[:, 2 * hd:3 * hd]), q, ones)
    v_new = _tile4(new[:, 3 * hd:4 * hd])
    ss, vs = [], []
    r = lax.broadcasted_iota(jnp.int32, (SEL_BLOCK, 1), 0)
    for i in range(nsel):
        page_t = blk_refs[i][0, 0].T
        half = idx_ref[b, i] % (page_t.shape[0] // SEL_BLOCK)
        blk = page_t[0:SEL_BLOCK]
        for hh in range(1, page_t.shape[0] // SEL_BLOCK):
            blk = jnp.where(half == hh, page_t[hh * SEL_BLOCK:(hh + 1) * SEL_BLOCK], blk)
        dist = (p0 - idx_ref[b, i] * SEL_BLOCK) - r
        ss.append(_head_scores(_tile4(blk[:, 0:hd]), q, ones) - slope * dist.astype(F32))
        vs.append(_tile4(blk[:, hd:2 * hd]))
    s = jnp.concatenate(ss, axis=0)
    v = jnp.concatenate(vs, axis=0)
    m = jnp.maximum(jnp.max(s, axis=0, keepdims=True), s_new)
    p, p_new = jnp.exp(s - m), jnp.exp(s_new - m)
    o_s = (jnp.sum(p * v, axis=0, keepdims=True) + p_new * v_new) / (jnp.sum(p, axis=0, keepdims=True) + p_new)
    win = win_ref[0, 0].T
    wn = wnew_ref[0]
    nw = win.shape[0]
    dist_w = nw - lax.broadcasted_iota(jnp.int32, (nw, 1), 0)
    ok = dist_w < WINDOW
    s_w = jnp.where(ok, _head_scores(_tile4(win[:, 0:hd]), q, ones) - slope * dist_w.astype(F32), NEG)
    sw_new = _head_scores(_tile4(wn[:, 0:hd]), q, ones)
    mw = jnp.maximum(jnp.max(s_w, axis=0, keepdims=True), sw_new)
    pw_, pw_new = jnp.where(ok, jnp.exp(s_w - mw), 0.0), jnp.exp(sw_new - mw)
    o_w = ((jnp.sum(pw_ * _tile4(win[:, hd:2 * hd]), axis=0, keepdims=True) + pw_new * _tile4(wn[:, hd:2 * hd]))
           / (jnp.sum(pw_, axis=0, keepdims=True) + pw_new))
    gt = jax.nn.sigmoid(small_ref[0])
    ge = [jnp.dot(gt, eg_ref[c], precision=HI, preferred_element_type=F32) for c in range(3)]
    o_ref[0] = _rms(ge[0] * oc_ref[0] + ge[1] * o_s + ge[2] * o_w, g_ref[...])


def _nsa_dec_sel(proj3, page_table, idx, o_c, cache, win_state, layer, slope_e, g, p0):
    bsz = proj3.shape[0]
    gw, hd, nh = GROUP_WIDTH, HEAD_DIM, N_HEADS
    nsel = SEL_TOPK - 1
    nw = win_state.shape[3]
    eg = np.zeros((3, 128, gw), np.float32)
    for c in range(3):
        for h in range(nh):
            eg[c, S_NGATE + 3 * h + c, h * hd:(h + 1) * hd] = 1.0
    n_pool = cache.shape[1]
    per_page = cache.shape[3] // SEL_BLOCK

    def blk(i):
        def index_map(b, pt, ix):
            sel = jnp.clip(ix[b, i], 0, p0 // SEL_BLOCK - 1)
            page = jnp.clip(pt[b, sel // per_page], 0, n_pool - 1)
            return (layer, page, 1, 0)
        return index_map
    const2 = lambda b, pt, ix: (0, 0)
    grid_spec = pltpu.PrefetchScalarGridSpec(
        num_scalar_prefetch=2,
        grid=(bsz,),
        in_specs=[pl.BlockSpec((1, 1, gw), lambda b, pt, ix: (b, 0, C_NQ // gw)),
                  pl.BlockSpec((1, 1, gw), lambda b, pt, ix: (b, 0, C_NROWS // gw)),
                  pl.BlockSpec((1, 1, 2 * hd), lambda b, pt, ix: (b, 0, C_WIN // (2 * hd))),
                  pl.BlockSpec((1, 1, 128), lambda b, pt, ix: (b, 0, C_SMALL // 128)),
                  pl.BlockSpec((1, 1, gw), lambda b, pt, ix: (b, 0, 0)),
                  pl.BlockSpec((1, 1, 2 * hd, nw), lambda b, pt, ix: (layer, b, 0, 0)),
                  pl.BlockSpec((1, gw), const2), pl.BlockSpec((gw, gw), const2),
                  pl.BlockSpec((3, 128, gw), lambda b, pt, ix: (0, 0, 0)), pl.BlockSpec((1, gw), const2)]
        + [pl.BlockSpec((1, 1, 2 * hd, cache.shape[3]), blk(i)) for i in range(nsel)],
        out_specs=pl.BlockSpec((1, 1, gw), lambda b, pt, ix: (b, 0, 0)),
    )
    return pl.pallas_call(
        functools.partial(_nsa_dec_sel_body, nsel=nsel, p0=p0),
        grid_spec=grid_spec,
        out_shape=jax.ShapeDtypeStruct((bsz, 1, gw), F32),
        compiler_params=_cparams(1),
        name="nsa_dec_sel",
    )(page_table, idx, proj3, proj3, proj3, proj3, o_c, win_state, slope_e, _head_ones(), jnp.asarray(eg), g,
      *([cache] * nsel))


def _gdn_dec_body(qkv_ref, z_ref, small_ref, cb_ref, s0_ref, cw_ref, al_ref, dt_ref, ng_ref, o_ref, s_ref):
    hd, nh, gw = HEAD_DIM, N_HEADS, GROUP_WIDTH
    cb = cb_ref[0, 0]
    conv = cb[0:1] * cw_ref[0:1, :]
    for t in range(1, CONV_W - 1):
        conv = conv + cb[t:t + 1] * cw_ref[t:t + 1, :]
    conv = conv + qkv_ref[0] * cw_ref[CONV_W - 1:CONV_W, :]
    conv = conv * jax.nn.sigmoid(conv)
    small = small_ref[0]
    g = -jnp.exp(al_ref[...]) * jax.nn.softplus(small[:, S_GA:S_GA + nh] + dt_ref[...])
    beta = jax.nn.sigmoid(small[:, S_GB:S_GB + nh])
    z = z_ref[0]
    ri = lax.broadcasted_iota(jnp.int32, (hd, hd), 0)
    ci = lax.broadcasted_iota(jnp.int32, (hd, hd), 1)
    outs = []
    for h in range(nh):
        q = _l2n(conv[:, h * hd:(h + 1) * hd]) * hd ** -0.5
        k = _l2n(conv[:, gw + h * hd:gw + (h + 1) * hd])
        v = conv[:, 2 * gw + h * hd:2 * gw + (h + 1) * hd]
        eg = jnp.exp(g[:, h:h + 1])
        bc = beta[:, h:h + 1]
        s = s0_ref[0, 0, h]
        v_new = v * bc - jnp.dot(k * bc * eg, s, precision=HI, preferred_element_type=F32)
        o = (jnp.dot(q * eg, s, precision=HI, preferred_element_type=F32)
             + jnp.sum(q * k, axis=-1, keepdims=True) * v_new)
        k_col = jnp.sum(jnp.where(ri == ci, jnp.broadcast_to(k, (hd, hd)), 0.0), axis=1, keepdims=True)
        s_ref[0, h] = s * eg + k_col * v_new
        o = o * lax.rsqrt(jnp.mean(o * o, axis=-1, keepdims=True) + RMS_EPS) * ng_ref[:, h * hd:(h + 1) * hd]
        zh = z[:, h * hd:(h + 1) * hd]
        outs.append(o * (zh * jax.nn.sigmoid(zh)))
    o_ref[0] = jnp.concatenate(outs, axis=1)


def _gdn_decode(proj3, conv_buf, s0, layer, conv_w, a_log, dt_bias, norm_g):
    bsz = proj3.shape[0]
    hd, nh, gw = HEAD_DIM, N_HEADS, GROUP_WIDTH
    const2 = lambda b: (0, 0)
    return pl.pallas_call(
        _gdn_dec_body,
        grid=(bsz,),
        in_specs=[pl.BlockSpec((1, 1, 3 * gw), lambda b: (b, 0, C_GQKV // (3 * gw))),
                  pl.BlockSpec((1, 1, gw), lambda b: (b, 0, C_GZ // gw)),
                  pl.BlockSpec((1, 1, 128), lambda b: (b, 0, C_SMALL // 128)),
                  pl.BlockSpec((1, 1, CONV_W - 1, 3 * gw), lambda b: (layer, b, 0, 0)),
                  pl.BlockSpec((1, 1, nh, hd, hd), lambda b: (layer, b, 0, 0, 0)),
                  pl.BlockSpec((CONV_W, 3 * gw), const2), pl.BlockSpec((1, nh), const2), pl.BlockSpec((1, nh), const2),
                  pl.BlockSpec((1, gw), const2)],
        out_specs=[pl.BlockSpec((1, 1, gw), lambda b: (b, 0, 0)), pl.BlockSpec((1, nh, hd, hd), lambda b: (b, 0, 0, 0))],
        out_shape=[jax.ShapeDtypeStruct((bsz, 1, gw), F32), jax.ShapeDtypeStruct((bsz, nh, hd, hd), F32)],
        compiler_params=_cparams(1),
        name="gdn_decode",
    )(proj3, proj3, proj3, conv_buf, s0, conv_w, a_log.reshape(1, nh), dt_bias.reshape(1, nh), norm_g)


def kernel(x_prompt, x_sample, cache_nsa_kv, cache_fox_kv, cache_fox_logf, state_nsa_win, state_gdn, state_gdn_conv, state_s5_re, state_s5_im, page_table, w_in, nsa_pool, s5_a_re, s5_a_im, s5_b_re, s5_b_im, s5_c_re, s5_c_im, s5_d, s5_log_dt, s5_w_glu, gdn_conv, gdn_a_log, gdn_dt_bias, fox_b_f, mix_norm, w_out, ln1_g, ln1_b, ln2_g, ln2_b, router_w, router_b, exp_w1, exp_w3, exp_w2):
    B, L, D = x_prompt.shape
    BS = x_sample.shape[0]
    hd, nh, gw = HEAD_DIM, N_HEADS, GROUP_WIDTH
    ns = S5_GROUPS * S5_STATE
    n_pool = cache_nsa_kv.shape[1]
    p0 = page_table.shape[1] * cache_nsa_kv.shape[2]
    slopes = 2.0 ** (-8.0 * (jnp.arange(nh, dtype=F32) + 1.0) / nh)
    slope_e = jnp.repeat(slopes, hd).reshape(1, gw)
    xp = x_prompt.reshape(B * L, D)
    xs = x_sample.reshape(BS, D)
    acc_p = [[] for _ in range(8)]
    acc_s = [[] for _ in range(8)]
    nsa_cache = cache_nsa_kv.transpose(0, 1, 3, 4, 2).reshape(DEPTH, n_pool, 4 * hd, -1)
    fox_cache_t = cache_fox_kv.transpose(0, 1, 3, 4, 5, 2).reshape(DEPTH, n_pool, 2 * gw, -1)
    fox_logf_t = cache_fox_logf.transpose(0, 1, 3, 2)
    win_all = state_nsa_win.transpose(0, 1, 3, 4, 2).reshape(DEPTH, BS, 2 * hd, -1)
    for l in range(DEPTH):
        wp_f = _permute_w_in(w_in[l])
        wo_f = w_out[l]
        wp, wo = wp_f.astype(BF16), wo_f.astype(BF16)
        w1 = exp_w1[l].astype(BF16).transpose(1, 0, 2).reshape(D, -1)
        w3 = exp_w3[l].astype(BF16).transpose(1, 0, 2).reshape(D, -1)
        w2 = exp_w2[l].astype(BF16).reshape(-1, D)
        rb = router_b.reshape(1, -1)
        ln1 = (ln1_g[l].reshape(1, D), ln1_b[l].reshape(1, D))
        ln2 = (ln2_g[l].reshape(1, D), ln2_b[l].reshape(1, D))
        g_nsa, g_s5, g_gdn, g_fox = [g.reshape(1, gw) for g in jnp.split(mix_norm[l], N_MIXERS)]
        s5_args = (s5_a_re[l], s5_a_im[l], s5_b_re[l], s5_b_im[l], s5_c_re[l], s5_c_im[l], s5_d[l],
                   s5_log_dt[l], s5_w_glu[l])
        prm, prm_f = _s5_params(*s5_args, BF16), _s5_params(*s5_args, F32)

        proj = _in_proj(xp, wp, 256)
        o_nsa = _nsa_prompt(proj, slopes, nsa_pool[l], g_nsa, B, L)
        small = proj[:, C_SMALL:C_SMALL + 128].reshape(B, L, 128)
        logf_t, cum_t = _fox_prep(small[:, :, S_FF:S_FF + nh].transpose(0, 2, 1), fox_b_f[l].reshape(nh, 1))
        o_fox = _fox_prompt(proj, cum_t.transpose(0, 2, 1), cum_t, g_fox, B, L, 256)
        u_tm = proj[:, C_S5U:C_S5U + gw].reshape(B, L, gw).transpose(1, 0, 2).reshape(L * B, gw)
        zst = jnp.zeros((B, ns), F32)
        o_s5, s5r, s5i = _s5(u_tm, prm, zst, zst, g_s5, B, L, 256)
        o_s5 = o_s5.reshape(L, B, gw).transpose(1, 0, 2).reshape(B * L, gw)
        a_t = jnp.pad(small[:, :, S_GA:S_GA + nh].transpose(0, 2, 1), ((0, 0), (0, 8 - nh), (0, 0)))
        o_gdn, gdn_st = _gdn_prompt(proj, a_t, jnp.zeros((B, 8, 3 * gw), F32), jnp.zeros((B, nh, hd, hd), F32),
                                    gdn_conv[l], gdn_a_log[l], gdn_dt_bias[l], g_gdn, B, L)
        xp = _out_proj_ln(xp, [o_nsa, o_s5, o_gdn, o_fox], wo, *ln1, 256)
        xp = _moe_ln(xp, router_w, rb, w1, w3, w2, *ln2, 256)
        st = (proj[:, C_NROWS:C_NROWS + gw].reshape(B, L, 4, hd),
              proj[:, C_FKV:C_FKV + 2 * gw].reshape(B, L, 2, nh, hd),
              logf_t.transpose(0, 2, 1),
              proj[:, C_WIN:C_WIN + 2 * hd].reshape(B, L, 2, hd)[:, L - min(WINDOW, L):],
              gdn_st,
              proj[:, C_GQKV:C_GQKV + 3 * gw].reshape(B, L, 3 * gw)[:, L - (CONV_W - 1):],
              s5r.reshape(B, S5_GROUPS, S5_STATE), s5i.reshape(B, S5_GROUPS, S5_STATE))
        for a, v in zip(acc_p, st):
            a.append(v)

        proj_s = _in_proj(xs, wp_f, BS)
        proj3 = proj_s.reshape(BS, 1, PROJ_COLS)
        pw = jnp.concatenate([jnp.broadcast_to(nsa_pool[l][0][:, None], (CMP_BLOCK, hd)),
                              jnp.broadcast_to(nsa_pool[l][1][:, None], (CMP_BLOCK, hd))], axis=1)
        o_c, idx = _nsa_dec_cmp(proj3, page_table, nsa_cache, l, jnp.tile(pw, (128 // CMP_BLOCK, 1)), slope_e, p0)
        win_state = state_nsa_win[l]
        o_nsa = _nsa_dec_sel(proj3, page_table, idx.reshape(BS, 128), o_c, nsa_cache, win_all, l, slope_e, g_nsa, p0)
        ff_s = proj_s[:, C_SMALL + S_FF:C_SMALL + S_FF + nh]
        o_fox, lfo = _fox_decode(proj_s[:, C_FQ:C_FQ + gw].reshape(BS, gw, 1),
                                 proj_s[:, C_FKV:C_FKV + 2 * gw].reshape(BS, 2 * gw, 1), ff_s.reshape(BS, nh, 1),
                                 page_table, fox_cache_t, fox_logf_t, l, fox_b_f[l].reshape(nh, 1),
                                 g_fox.reshape(gw, 1))
        o_s5, s5r, s5i = _s5(proj_s[:, C_S5U:C_S5U + gw], prm_f, state_s5_re[l].reshape(BS, ns),
                             state_s5_im[l].reshape(BS, ns), g_s5, BS, 1, 1)
        o_gdn, gdn_st = _gdn_decode(proj3, state_gdn_conv, state_gdn, l, gdn_conv[l], gdn_a_log[l],
                                    gdn_dt_bias[l], g_gdn)
        xs = _out_proj_ln(xs, [o_nsa.reshape(BS, gw), o_s5, o_gdn.reshape(BS, gw), o_fox.reshape(BS, gw)], wo_f, *ln1, BS)
        xs = _moe_dec(xs, router_w, rb, exp_w1, exp_w3, exp_w2, l, *ln2)
        st = (proj_s[:, C_NROWS:C_NROWS + gw].reshape(BS, 1, 4, hd),
              proj_s[:, C_FKV:C_FKV + 2 * gw].reshape(BS, 1, 2, nh, hd),
              lfo.reshape(BS, 1, nh),
              jnp.concatenate([win_state[:, 1:], proj_s[:, C_WIN:C_WIN + 2 * hd].reshape(BS, 1, 2, hd)], axis=1),
              gdn_st,
              jnp.concatenate([state_gdn_conv[l][:, 1:], proj_s[:, C_GQKV:C_GQKV + 3 * gw].reshape(BS, 1, 3 * gw)], axis=1),
              s5r.reshape(BS, S5_GROUPS, S5_STATE), s5i.reshape(BS, S5_GROUPS, S5_STATE))
        for a, v in zip(acc_s, st):
            a.append(v)
    nsa_rows_p, fox_kv_p, fox_logf_p, nsa_win_p, gdn_p, gdn_conv_p, s5_re_p, s5_im_p = [jnp.stack(a) for a in acc_p]
    nsa_rows_s, fox_kv_s, fox_logf_s, nsa_win_s, gdn_s, gdn_conv_s, s5_re_s, s5_im_s = [jnp.stack(a) for a in acc_s]
    return (xp.reshape(B, L, D), xs.reshape(BS, 1, D), nsa_rows_p, nsa_rows_s, fox_kv_p, fox_kv_s, fox_logf_p, fox_logf_s,
            nsa_win_p, nsa_win_s, gdn_p, gdn_s, gdn_conv_p, gdn_conv_s, s5_re_p, s5_re_s, s5_im_p, s5_im_s)
```

```python
import functools
import math

import numpy as np
import jax
import jax.numpy as jnp
from jax import lax
from jax.experimental import pallas as pl
from jax.experimental.pallas import tpu as pltpu

F32 = jnp.float32
BF16 = jnp.bfloat16
HI = lax.Precision.HIGHEST

DEPTH = 4
N_MIXERS = 4
GROUP_WIDTH = 256
HEAD_DIM = 64
N_HEADS = 4
CMP_BLOCK = 32
SEL_BLOCK = 64
SEL_TOPK = 16
WINDOW = 512
FORCE_SCORE = 1.0e4
S5_GROUP = 16
S5_GROUPS = 16
S5_STATE = 64
CONV_W = 4
GDN_CHUNK = 64
N_EXPERTS = 16
N_EXPERT_GROUPS = 4
EXPERTS_PER_GROUP = 4
D_EXPERT = 256
ALPHA = (2.0 * DEPTH) ** 0.25
LN_EPS = 1e-5
RMS_EPS = 1e-6
NEG = -1e30

VMEM_LIMIT = 56 * 1024 * 1024

C_GQKV, C_NQ, C_FKV, C_NROWS, C_S5U, C_GZ, C_FQ, C_WIN, C_SMALL = 0, 768, 1024, 1536, 1792, 2048, 2304, 2560, 2688
PROJ_COLS = 2816
S_NGATE, S_GA, S_GB, S_FF = 0, 12, 16, 20


_IN_PROJ_SEGS = ((908, 1676), (0, 256), (2196, 2708), (256, 512), (652, 908), (1684, 1940), (1940, 2196),
                 (512, 640), (640, 652), (1676, 1680), (1680, 1684), (2708, 2712))


def _in_proj_perm():
    return np.concatenate([np.arange(a, b) for a, b in _IN_PROJ_SEGS])


def _permute_w_in(w):
    used = sum(b - a for a, b in _IN_PROJ_SEGS)
    parts = [w[:, a:b] for a, b in _IN_PROJ_SEGS] + [jnp.zeros((w.shape[0], PROJ_COLS - used), w.dtype)]
    return jnp.concatenate(parts, axis=1)


def _dotw(a, w):
    if w.dtype == BF16:
        return jnp.dot(a.astype(BF16), w, preferred_element_type=F32)
    return jnp.dot(a, w, precision=HI, preferred_element_type=F32)


def _cparams(n_axes):
    return pltpu.CompilerParams(dimension_semantics=("arbitrary",) * n_axes, vmem_limit_bytes=VMEM_LIMIT)


def _resident(shape):
    nd = len(shape)
    return pl.BlockSpec(shape, lambda *_: (0,) * nd, pipeline_mode=pl.Buffered(1))


def _dot_nt(a, b):
    return lax.dot_general(a, b, (((1,), (1,)), ((), ())), preferred_element_type=F32)


def _dot_tn(a, b):
    return lax.dot_general(a, b, (((0,), (0,)), ((), ())), preferred_element_type=F32)


def _rms(x, g):
    return x * lax.rsqrt(jnp.mean(x * x, axis=-1, keepdims=True) + RMS_EPS) * g


def _ln(x, g, b):
    mu = jnp.mean(x, axis=-1, keepdims=True)
    xc = x - mu
    var = jnp.mean(xc * xc, axis=-1, keepdims=True)
    return xc * lax.rsqrt(var + LN_EPS) * g + b


def _in_proj_body(x_ref, w_ref, o_ref, *, nch):
    x = x_ref[...]
    x = x.astype(BF16) if w_ref.dtype == BF16 else x
    for j in range(0, o_ref.shape[1], nch):
        o_ref[:, j:j + nch] = _dotw(x, w_ref[:, j:j + nch])


def _in_proj(x2d, w, tm):
    m, k = x2d.shape
    n = w.shape[1]
    return pl.pallas_call(
        functools.partial(_in_proj_body, nch=256),
        grid=(m // tm,),
        in_specs=[pl.BlockSpec((tm, k), lambda i: (i, 0)), _resident((k, n))],
        out_specs=pl.BlockSpec((tm, n), lambda i: (i, 0)),
        out_shape=jax.ShapeDtypeStruct((m, n), F32),
        compiler_params=_cparams(1),
        name="in_proj",
    )(x2d, w)


def _out_proj_body(x_ref, o0_ref, o1_ref, o2_ref, o3_ref, w_ref, g_ref, b_ref, out_ref):
    gw = GROUP_WIDTH
    y = _dotw(o0_ref[...], w_ref[0:gw, :])
    y = y + _dotw(o1_ref[...], w_ref[gw:2 * gw, :])
    y = y + _dotw(o2_ref[...], w_ref[2 * gw:3 * gw, :])
    y = y + _dotw(o3_ref[...], w_ref[3 * gw:4 * gw, :])
    out_ref[...] = _ln(ALPHA * x_ref[...] + y, g_ref[...], b_ref[...])


def _out_proj_ln(x2d, mixers, w_out, g, b, tm):
    m, d = x2d.shape
    gw = GROUP_WIDTH
    row = lambda i: (i, 0)
    return pl.pallas_call(
        _out_proj_body,
        grid=(m // tm,),
        in_specs=[pl.BlockSpec((tm, d), row)] + [pl.BlockSpec((tm, gw), row)] * 4
        + [_resident((d, d)), _resident((1, d)), _resident((1, d))],
        out_specs=pl.BlockSpec((tm, d), row),
        out_shape=jax.ShapeDtypeStruct((m, d), F32),
        compiler_params=_cparams(1),
        name="out_proj_ln",
    )(x2d, *mixers, w_out, g, b)


def _moe_gate_t(lt):
    m = jnp.max(lt, axis=0, keepdims=True)
    p = jnp.exp(lt - m)
    probs = p / jnp.sum(p, axis=0, keepdims=True)
    rows = [probs[e:e + 1, :] for e in range(N_EXPERTS)]
    n = EXPERTS_PER_GROUP
    scores = []
    for g in range(N_EXPERT_GROUPS):
        r = rows[g * n:(g + 1) * n]
        best = None
        for i in range(n):
            for j in range(i + 1, n):
                s = r[i] + r[j]
                best = s if best is None else jnp.maximum(best, s)
        scores.append(best)
    grp = jnp.zeros_like(scores[0], dtype=jnp.int32)
    top = scores[0]
    for g in range(1, N_EXPERT_GROUPS):
        take = scores[g] > top
        grp = jnp.where(take, g, grp)
        top = jnp.where(take, scores[g], top)
    vals = []
    for j in range(n):
        v = rows[j]
        for g in range(1, N_EXPERT_GROUPS):
            v = jnp.where(grp == g, rows[g * n + j], v)
        vals.append(v)

    def first_argmax(vs):
        idx = jnp.zeros_like(grp)
        best = vs[0]
        for j in range(1, n):
            take = vs[j] > best
            idx = jnp.where(take, j, idx)
            best = jnp.where(take, vs[j], best)
        return best, idx

    v1, i1 = first_argmax(vals)
    v2, i2 = first_argmax([jnp.where(i1 == j, -jnp.inf, vals[j]) for j in range(n)])
    tot = v1 + v2
    w1, w2 = v1 / tot, v2 / tot
    e1, e2 = grp * n + i1, grp * n + i2
    gate = [jnp.where(e1 == e, w1, 0.0) + jnp.where(e2 == e, w2, 0.0) for e in range(N_EXPERTS)]
    return jnp.concatenate(gate, axis=0)


def _moe_body(x_ref, rw_ref, rb_ref, w1_ref, w3_ref, w2_ref, g_ref, b_ref, out_ref, *, epc):
    x = x_ref[...]
    tm = x.shape[0]
    logits = jnp.dot(x, rw_ref[...], precision=HI, preferred_element_type=F32) + rb_ref[...]
    gate = _moe_gate_t(logits.T).T
    xb = x.astype(BF16)
    cw = epc * D_EXPERT
    y = jnp.zeros((tm, x.shape[1]), F32)
    for c in range(N_EXPERTS // epc):
        a = jnp.dot(xb, w1_ref[:, c * cw:(c + 1) * cw], preferred_element_type=F32)
        b = jnp.dot(xb, w3_ref[:, c * cw:(c + 1) * cw], preferred_element_type=F32)
        ge = jnp.concatenate(
            [jnp.broadcast_to(gate[:, e:e + 1], (tm, D_EXPERT)) for e in range(c * epc, (c + 1) * epc)], axis=1)
        h = (a * jax.nn.sigmoid(a)) * b * ge
        y = y + jnp.dot(h.astype(BF16), w2_ref[c * cw:(c + 1) * cw, :], preferred_element_type=F32)
    out_ref[...] = _ln(ALPHA * x + y, g_ref[...], b_ref[...])


def _moe_ln(x2d, rw, rb, w1, w3, w2, g, b, tm):
    m, d = x2d.shape
    ne = N_EXPERTS * D_EXPERT
    row = lambda i: (i, 0)
    return pl.pallas_call(
        functools.partial(_moe_body, epc=4),
        grid=(m // tm,),
        in_specs=[pl.BlockSpec((tm, d), row), _resident((d, N_EXPERTS)), _resident((1, N_EXPERTS)),
                  _resident((d, ne)), _resident((d, ne)), _resident((ne, d)), _resident((1, d)), _resident((1, d))],
        out_specs=pl.BlockSpec((tm, d), row),
        out_shape=jax.ShapeDtypeStruct((m, d), F32),
        compiler_params=_cparams(1),
        name="moe_ln",
    )(x2d, rw, rb, w1, w3, w2, g, b)


def _moe_dec_body(x_ref, rw_ref, rb_ref, w1_ref, w3_ref, w2_ref, g_ref, b_ref, out_ref, ge_ref, acc_ref, *, epc):
    c = pl.program_id(0)
    x = x_ref[...]
    tm = x.shape[0]

    @pl.when(c == 0)
    def _():
        logits = jnp.dot(x, rw_ref[...], precision=HI, preferred_element_type=F32) + rb_ref[...]
        gate = _moe_gate_t(logits.T).T
        for cc in range(N_EXPERTS // epc):
            ge_ref[cc] = jnp.concatenate(
                [jnp.broadcast_to(gate[:, e:e + 1], (tm, D_EXPERT)) for e in range(cc * epc, (cc + 1) * epc)], axis=1)
        acc_ref[...] = jnp.zeros(acc_ref.shape, F32)

    ge = ge_ref[c]
    y = acc_ref[...]
    for e in range(epc):
        a = _dotw(x, w1_ref[0, e])
        b = _dotw(x, w3_ref[0, e])
        h = (a * jax.nn.sigmoid(a)) * b * ge[:, e * D_EXPERT:(e + 1) * D_EXPERT]
        y = y + _dotw(h, w2_ref[0, e])
    acc_ref[...] = y

    @pl.when(c == pl.num_programs(0) - 1)
    def _():
        out_ref[...] = _ln(ALPHA * x + acc_ref[...], g_ref[...], b_ref[...])


def _moe_dec(x2d, rw, rb, w1, w3, w2, layer, g, b, epc=2):
    m, d = x2d.shape
    cw = epc * D_EXPERT
    nchunk = N_EXPERTS // epc
    const = lambda c: (0, 0)
    return pl.pallas_call(
        functools.partial(_moe_dec_body, epc=epc),
        grid=(nchunk,),
        in_specs=[pl.BlockSpec((m, d), const), pl.BlockSpec((d, N_EXPERTS), const), pl.BlockSpec((1, N_EXPERTS), const),
                  pl.BlockSpec((1, epc, d, D_EXPERT), lambda c: (layer, c, 0, 0)),
                  pl.BlockSpec((1, epc, d, D_EXPERT), lambda c: (layer, c, 0, 0)),
                  pl.BlockSpec((1, epc, D_EXPERT, d), lambda c: (layer, c, 0, 0)), pl.BlockSpec((1, d), const),
                  pl.BlockSpec((1, d), const)],
        out_specs=pl.BlockSpec((m, d), const),
        out_shape=jax.ShapeDtypeStruct((m, d), F32),
        scratch_shapes=[pltpu.VMEM((nchunk, m, cw), F32), pltpu.VMEM((m, d), F32)],
        compiler_params=_cparams(1),
        name="moe_dec",
    )(x2d, rw, rb, w1, w3, w2, g, b)


def _fox_prep_body(ff_ref, bf_ref, logf_ref, cum_ref):
    lf = jax.nn.log_sigmoid(ff_ref[0] + bf_ref[...])
    logf_ref[0] = lf
    n = lf.shape[1]
    lane = lax.broadcasted_iota(jnp.int32, lf.shape, 1)
    c = lf
    s = 1
    while s < n:
        c = c + jnp.where(lane >= s, pltpu.roll(c, s, 1), 0.0)
        s *= 2
    cum_ref[0] = c


def _fox_prep(ff_t, bf):
    bsz, nh, n = ff_t.shape
    blk = pl.BlockSpec((1, nh, n), lambda b: (b, 0, 0))
    return pl.pallas_call(
        _fox_prep_body,
        grid=(bsz,),
        in_specs=[blk, _resident((nh, 1))],
        out_specs=[blk, blk],
        out_shape=[jax.ShapeDtypeStruct((bsz, nh, n), F32)] * 2,
        compiler_params=_cparams(1),
        name="fox_prep",
    )(ff_t, bf)


def _fox_body(q_ref, kv_ref, cq_ref, ck_ref, g_ref, o_ref, m_ref, l_ref, acc_ref, *, tq, tk):
    qi = pl.program_id(1)
    hd, nh, gw = HEAD_DIM, N_HEADS, GROUP_WIDTH
    scale = hd ** -0.5
    q = q_ref[...] * scale
    hs = range(nh)
    qb = [q[:, h * hd:(h + 1) * hd].astype(BF16) for h in hs]
    cq = cq_ref[0]
    cqs = [cq[:, h:h + 1] for h in hs]
    m_ref[...] = jnp.full(m_ref.shape, NEG, F32)
    l_ref[...] = jnp.zeros(l_ref.shape, F32)
    acc_ref[...] = jnp.zeros(acc_ref.shape, F32)
    q0 = qi * tq
    t_q = q0 + lax.broadcasted_iota(jnp.int32, (tq, tk), 0)
    lane_k = lax.broadcasted_iota(jnp.int32, (tq, tk), 1)

    def chunk(kc, last):
        r0 = pl.multiple_of(kc * tk, tk)
        kv = kv_ref[pl.ds(r0, tk), :]
        ck = ck_ref[0, :, pl.ds(r0, tk)]
        m_old = [m_ref[h] for h in hs]
        l_old = [l_ref[h] for h in hs]
        acc_old = [acc_ref[h] for h in hs]
        s = [_dot_nt(qb[h], kv[:, h * hd:(h + 1) * hd].astype(BF16)) + cqs[h] - ck[h:h + 1, :] for h in hs]
        if last:
            bias = jnp.where(r0 + lane_k <= t_q, 0.0, NEG)
            s = [x + bias for x in s]
        m_new = [jnp.maximum(m_old[h], jnp.max(s[h], axis=1, keepdims=True)) for h in hs]
        a = [jnp.exp(m_old[h] - m_new[h]) for h in hs]
        p = [jnp.exp(s[h] - m_new[h]) for h in hs]
        l_new = [a[h] * l_old[h] + jnp.sum(p[h], axis=1, keepdims=True) for h in hs]
        pv = [jnp.dot(p[h].astype(BF16), kv[:, gw + h * hd:gw + (h + 1) * hd].astype(BF16),
                      preferred_element_type=F32) for h in hs]
        for h in hs:
            m_ref[h] = m_new[h]
            l_ref[h] = l_new[h]
            acc_ref[h] = a[h] * acc_old[h] + pv[h]

    def body(kc, carry):
        chunk(kc, False)
        return carry

    n_full = q0 // tk
    lax.fori_loop(0, n_full, body, 0)
    chunk(n_full, True)
    o_ref[...] = _rms(jnp.concatenate([acc_ref[h] / l_ref[h] for h in hs], axis=1), g_ref[...])


def _fox_prompt(proj, cum_t, g, bsz, n, tq, tk=512):
    nh, gw = N_HEADS, GROUP_WIDTH
    nq = n // tq
    assert tk % tq == 0 and n % tk == 0
    return pl.pallas_call(
        functools.partial(_fox_body, tq=tq, tk=tk),
        grid=(bsz, nq),
        in_specs=[pl.BlockSpec((tq, gw), lambda b, i: (b * nq + i, C_FQ // gw)),
                  pl.BlockSpec((n, 2 * gw), lambda b, i: (b, C_FKV // (2 * gw))),
                  pl.BlockSpec((1, tq, nh), lambda b, i: (b, i, 0)),
                  pl.BlockSpec((1, nh, n), lambda b, i: (b, 0, 0)),
                  _resident((1, gw))],
        out_specs=pl.BlockSpec((tq, gw), lambda b, i: (b * nq + i, 0)),
        out_shape=jax.ShapeDtypeStruct((bsz * n, gw), F32),
        scratch_shapes=[pltpu.VMEM((nh, tq, 1), F32), pltpu.VMEM((nh, tq, 1), F32),
                        pltpu.VMEM((nh, tq, HEAD_DIM), F32)],
        compiler_params=_cparams(2),
        name="fox_prompt",
    )(proj, proj, cum_t.transpose(0, 2, 1), cum_t, g)


def _slope_col(slopes_ref, tq):
    hrow = lax.broadcasted_iota(jnp.int32, (N_HEADS * tq, 1), 0) // tq
    s = jnp.full((N_HEADS * tq, 1), slopes_ref[0], F32)
    for h in range(1, N_HEADS):
        s = jnp.where(hrow == h, slopes_ref[h], s)
    return s


def _softmax_rows(s, mask):
    s = jnp.where(mask, s, NEG)
    m = jnp.max(s, axis=1, keepdims=True)
    p = jnp.where(mask, jnp.exp(s - m), 0.0)
    return p / jnp.maximum(jnp.sum(p, axis=1, keepdims=True), 1e-30)


def _nsa_body(slopes_ref, q_ref, rows_ref, win_ref, small_ref, pw_ref, pair_ref, exp_ref, g_ref, o_ref,
              kvc_ref, m_ref, l_ref, acc_ref, *, tq, tk, n):
    qi = pl.program_id(1)
    hd, nh = HEAD_DIM, N_HEADS
    scale = hd ** -0.5
    n_cmp, n_sel = n // CMP_BLOCK, n // SEL_BLOCK
    r4 = nh * tq

    @pl.when(qi == 0)
    def _():
        kv = rows_ref[:, 0:2 * hd].reshape(n_cmp, CMP_BLOCK, 2 * hd)
        kvc_ref[...] = jnp.sum(kv * pw_ref[...][None], axis=1)

    q = q_ref[...] * scale
    qs = jnp.concatenate([q[:, h * hd:(h + 1) * hd] for h in range(nh)], axis=0).astype(BF16)
    slope = _slope_col(slopes_ref, tq)
    q0 = qi * tq
    t_row = q0 + lax.broadcasted_iota(jnp.int32, (r4, 1), 0) % tq
    t_q = q0 + lax.broadcasted_iota(jnp.int32, (tq, 1), 0)

    kvc = kvc_ref[...]
    s_c = _dot_nt(qs, kvc[:, 0:hd].astype(BF16))
    cmp_end = (lax.broadcasted_iota(jnp.int32, (1, n_cmp), 1) + 1) * CMP_BLOCK - 1
    dist_c = t_row - cmp_end
    p_c = _softmax_rows(s_c - slope * dist_c.astype(F32), dist_c >= 0)
    o_c = jnp.dot(p_c.astype(BF16), kvc[:, hd:2 * hd].astype(BF16), preferred_element_type=F32)

    psum = p_c[0:tq]
    for h in range(1, nh):
        psum = psum + p_c[h * tq:(h + 1) * tq]
    imp = lax.dot_general(pair_ref[...], psum, (((1,), (1,)), ((), ())), precision=HI,
                          preferred_element_type=F32)
    blk = lax.broadcasted_iota(jnp.int32, (n_sel, tq), 0)
    cur = (q0 + lax.broadcasted_iota(jnp.int32, (1, tq), 1)) // SEL_BLOCK
    valid = blk <= cur
    forced = (blk == cur) | (blk == 0)
    score = jnp.where(valid, jnp.where(forced, FORCE_SCORE, imp), -jnp.inf)
    rank = jnp.zeros((n_sel, tq), jnp.int32)
    for j in range(n_sel):
        sj = score[j:j + 1, :]
        beats = (sj > score) | ((sj == score) & (blk > j))
        rank = rank + beats.astype(jnp.int32)
    sel = ((rank < SEL_TOPK) & valid).astype(BF16)

    far = 1e9

    m_ref[...] = jnp.full(m_ref.shape, NEG, F32)
    l_ref[...] = jnp.zeros(l_ref.shape, F32)
    acc_ref[...] = jnp.zeros(acc_ref.shape, F32)
    lane_k = lax.broadcasted_iota(jnp.int32, (1, tk), 1)

    def body(kc, carry):
        r0 = pl.multiple_of(kc * tk, tk)
        kvs = rows_ref[pl.ds(r0, tk), 2 * hd:4 * hd]
        ks, vs = kvs[:, 0:hd].astype(BF16), kvs[:, hd:2 * hd].astype(BF16)
        dist = t_q - (r0 + lane_k)
        keep = _dot_tn(sel, exp_ref[kc])
        dmask = jnp.where((dist >= 0) & (keep > 0.5), dist.astype(F32), far)
        s = _dot_nt(qs, ks) - slope * jnp.concatenate([dmask] * nh, axis=0)
        m_old = m_ref[...]
        m_new = jnp.maximum(m_old, jnp.max(s, axis=1, keepdims=True))
        a = jnp.exp(m_old - m_new)
        p = jnp.exp(s - m_new)
        l_ref[...] = a * l_ref[...] + jnp.sum(p, axis=1, keepdims=True)
        acc_ref[...] = a * acc_ref[...] + jnp.dot(p.astype(BF16), vs, preferred_element_type=F32)
        m_ref[...] = m_new
        return carry

    lax.fori_loop(0, (q0 + tq + tk - 1) // tk, body, 0)
    o_s = acc_ref[...] / l_ref[...]

    wlen = WINDOW + tq
    w0 = pl.multiple_of(jnp.maximum(q0 - WINDOW, 0), tq)
    wkv = win_ref[pl.ds(w0, wlen), :]
    dist_w = t_q - (w0 + lax.broadcasted_iota(jnp.int32, (1, wlen), 1))
    dmask_w = jnp.where((dist_w >= 0) & (dist_w < WINDOW), dist_w.astype(F32), far)
    s_w = _dot_nt(qs, wkv[:, 0:hd].astype(BF16)) - slope * jnp.concatenate([dmask_w] * nh, axis=0)
    p_w = jnp.exp(s_w - jnp.max(s_w, axis=1, keepdims=True))
    o_w = (jnp.dot(p_w.astype(BF16), wkv[:, hd:2 * hd].astype(BF16), preferred_element_type=F32)
           / jnp.sum(p_w, axis=1, keepdims=True))

    gt = jax.nn.sigmoid(small_ref[:, S_NGATE:S_NGATE + 3 * nh])
    outs = []
    for h in range(nh):
        sl = slice(h * tq, (h + 1) * tq)
        outs.append(gt[:, 3 * h:3 * h + 1] * o_c[sl] + gt[:, 3 * h + 1:3 * h + 2] * o_s[sl]
                    + gt[:, 3 * h + 2:3 * h + 3] * o_w[sl])
    o_ref[...] = _rms(jnp.concatenate(outs, axis=1), g_ref[...])


def _nsa_prompt(proj, slopes, pool_w, g, bsz, n, tq=128, tk=512):
    hd, nh, gw = HEAD_DIM, N_HEADS, GROUP_WIDTH
    assert n % tk == 0 and n >= WINDOW + tq and n % SEL_BLOCK == 0
    nq = n // tq
    n_cmp, n_sel = n // CMP_BLOCK, n // SEL_BLOCK
    pw = jnp.concatenate([jnp.broadcast_to(pool_w[0][:, None], (CMP_BLOCK, hd)),
                          jnp.broadcast_to(pool_w[1][:, None], (CMP_BLOCK, hd))], axis=1)
    pair = (np.arange(n_sel)[:, None] == np.arange(n_cmp)[None, :] // 2).astype(np.float32)
    expand = (np.arange(n_sel)[:, None] == np.arange(n)[None, :] // SEL_BLOCK)
    expand = jnp.asarray(expand.reshape(n_sel, n // tk, tk).transpose(1, 0, 2), BF16)
    return pl.pallas_call(
        functools.partial(_nsa_body, tq=tq, tk=tk, n=n),
        grid=(bsz, nq),
        in_specs=[pl.BlockSpec(memory_space=pltpu.SMEM),
                  pl.BlockSpec((tq, gw), lambda b, i: (b * nq + i, C_NQ // gw)),
                  pl.BlockSpec((n, gw), lambda b, i: (b, C_NROWS // gw)),
                  pl.BlockSpec((n, 2 * hd), lambda b, i: (b, C_WIN // (2 * hd))),
                  pl.BlockSpec((tq, 128), lambda b, i: (b * nq + i, C_SMALL // 128)),
                  _resident((CMP_BLOCK, 2 * hd)), _resident((n_sel, n_cmp)), _resident((n // tk, n_sel, tk)),
                  _resident((1, gw))],
        out_specs=pl.BlockSpec((tq, gw), lambda b, i: (b * nq + i, 0)),
        out_shape=jax.ShapeDtypeStruct((bsz * n, gw), F32),
        scratch_shapes=[pltpu.VMEM((n_cmp, 2 * hd), F32), pltpu.VMEM((nh * tq, 1), F32),
                        pltpu.VMEM((nh * tq, 1), F32), pltpu.VMEM((nh * tq, hd), F32)],
        compiler_params=_cparams(2),
        name="nsa_prompt",
    )(slopes, proj, proj, proj, proj, pw, jnp.asarray(pair), expand, g)


def _s5_body(u_ref, bre_ref, bim_ref, are_ref, aim_ref, s0r_ref, s0i_ref, cre_ref, cim_ref, d_ref, wg_ref, g_ref,
             o_ref, fr_ref, fi_ref, xr_ref, xi_ref, sr_ref, si_ref, *, bsz, tc):
    @pl.when(pl.program_id(0) == 0)
    def _():
        sr_ref[...] = s0r_ref[...]
        si_ref[...] = s0i_ref[...]

    u = u_ref[...]
    xr_ref[...] = _dotw(u, bre_ref[...])
    xi_ref[...] = _dotw(u, bim_ref[...])
    ar = jnp.broadcast_to(are_ref[...], sr_ref.shape)
    ai = jnp.broadcast_to(aim_ref[...], sr_ref.shape)

    def step(t, carry):
        sr, si = carry
        r0 = pl.multiple_of(t * bsz, bsz)
        nr = ar * sr - ai * si + xr_ref[pl.ds(r0, bsz), :]
        ni = ar * si + ai * sr + xi_ref[pl.ds(r0, bsz), :]
        xr_ref[pl.ds(r0, bsz), :] = nr
        xi_ref[pl.ds(r0, bsz), :] = ni
        return nr, ni

    sr, si = lax.fori_loop(0, tc, step, (sr_ref[...], si_ref[...]))
    sr_ref[...] = sr
    si_ref[...] = si
    fr_ref[...] = sr
    fi_ref[...] = si
    y = _dotw(xr_ref[...], cre_ref[...]) - _dotw(xi_ref[...], cim_ref[...])
    y = jax.nn.gelu(y + d_ref[...] * u)
    out = y * jax.nn.sigmoid(_dotw(y, wg_ref[...]))
    o_ref[...] = _rms(out, g_ref[...])


def _s5(u_tm, prm, s0r, s0i, g, bsz, n, tc):
    gw = GROUP_WIDTH
    ns = S5_GROUPS * S5_STATE
    rows = tc * bsz
    st = jax.ShapeDtypeStruct((bsz, ns), F32)
    return pl.pallas_call(
        functools.partial(_s5_body, bsz=bsz, tc=tc),
        grid=(n // tc,),
        in_specs=[pl.BlockSpec((rows, gw), lambda i: (i, 0)),
                  _resident((gw, ns)), _resident((gw, ns)), _resident((1, ns)), _resident((1, ns)),
                  _resident((bsz, ns)), _resident((bsz, ns)), _resident((ns, gw)), _resident((ns, gw)),
                  _resident((1, gw)), _resident((gw, gw)), _resident((1, gw))],
        out_specs=[pl.BlockSpec((rows, gw), lambda i: (i, 0)), pl.BlockSpec((bsz, ns), lambda i: (0, 0)),
                   pl.BlockSpec((bsz, ns), lambda i: (0, 0))],
        out_shape=[jax.ShapeDtypeStruct((n * bsz, gw), F32), st, st],
        scratch_shapes=[pltpu.VMEM((rows, ns), F32), pltpu.VMEM((rows, ns), F32),
                        pltpu.VMEM((bsz, ns), F32), pltpu.VMEM((bsz, ns), F32)],
        compiler_params=_cparams(1),
        name="s5",
    )(u_tm, prm["bre"], prm["bim"], prm["are"], prm["aim"], s0r, s0i, prm["cre"], prm["cim"], prm["d"], prm["wg"], g)


def _s5_params(a_re, a_im, b_re, b_im, c_re, c_im, d, log_dt, w_glu, wdt):
    dt = jnp.exp(log_dt)[:, None]
    mag = jnp.exp(dt * a_re)
    abar_re, abar_im = mag * jnp.cos(dt * a_im), mag * jnp.sin(dt * a_im)
    den = a_re * a_re + a_im * a_im
    num_re, num_im = abar_re - 1.0, abar_im
    zoh_re = (num_re * a_re + num_im * a_im) / den
    zoh_im = (num_im * a_re - num_re * a_im) / den
    bbar_re = zoh_re[..., None] * b_re - zoh_im[..., None] * b_im
    bbar_im = zoh_re[..., None] * b_im + zoh_im[..., None] * b_re
    eye = jnp.eye(S5_GROUPS, dtype=F32)
    ns = S5_GROUPS * S5_STATE
    to_in = lambda w: (w[:, :, None, :] * eye[:, None, :, None]).transpose(0, 3, 2, 1).reshape(GROUP_WIDTH, ns).astype(wdt)
    to_out = lambda w: (w[:, :, None, :] * eye[:, None, :, None]).transpose(0, 3, 2, 1).reshape(ns, GROUP_WIDTH).astype(wdt)
    return {"bre": to_in(bbar_re), "bim": to_in(bbar_im), "are": abar_re.reshape(1, ns), "aim": abar_im.reshape(1, ns),
            "cre": to_out(c_re), "cim": to_out(c_im), "d": d.reshape(1, GROUP_WIDTH), "wg": w_glu.astype(wdt)}


def _mm_bf16(a, b):
    return jnp.dot(a.astype(BF16), b.astype(BF16), preferred_element_type=F32)


def _l2n(x):
    return x * lax.rsqrt(jnp.sum(x * x, axis=-1, keepdims=True) + RMS_EPS)


def _gdn_body(qkv_ref, z_ref, small_ref, at_ref, cb_ref, s0_ref, cw_ref, alr_ref, dtr_ref, alc_ref, dtc_ref, ng_ref,
              o_ref, sfin_ref, ext_ref, s_ref, *, tc):
    j = pl.program_id(1)
    hd, nh, gw, ch = HEAD_DIM, N_HEADS, GROUP_WIDTH, GDN_CHUNK
    pad = 8

    @pl.when(j == 0)
    def _():
        ext_ref[0:pad, :] = cb_ref[0]
        s_ref[...] = s0_ref[0]

    @pl.when(j > 0)
    def _():
        ext_ref[0:pad, :] = ext_ref[tc:tc + pad, :]

    ext_ref[pad:tc + pad, :] = qkv_ref[...]
    base = pad - (CONV_W - 1)
    conv = ext_ref[base:base + tc, :] * cw_ref[0:1, :]
    for t in range(1, CONV_W):
        conv = conv + ext_ref[base + t:base + t + tc, :] * cw_ref[t:t + 1, :]
    conv = conv * jax.nn.sigmoid(conv)

    small = small_ref[...]
    g_col = -jnp.exp(alr_ref[...]) * jax.nn.softplus(small[:, S_GA:S_GA + nh] + dtr_ref[...])
    beta = jax.nn.sigmoid(small[:, S_GB:S_GB + nh])
    g_row = -jnp.exp(alc_ref[...]) * jax.nn.softplus(at_ref[0][0:nh, :] + dtc_ref[...])
    ri = lax.broadcasted_iota(jnp.int32, (tc, tc), 0)
    ci = lax.broadcasted_iota(jnp.int32, (tc, tc), 1)
    same = (ri // ch) == (ci // ch)
    tril = same & (ci <= ri)
    strict = same & (ci < ri)
    tril_f = tril.astype(F32)
    cum_col = jnp.dot(tril_f, g_col, precision=HI, preferred_element_type=F32)
    cum_row = lax.dot_general(g_row, tril_f, (((1,), (1,)), ((), ())), precision=HI,
                              preferred_element_type=F32)
    z = z_ref[...]
    hs = range(nh)
    gc = [cum_col[:, h:h + 1] for h in hs]
    decay = [jnp.where(tril, jnp.exp(jnp.where(tril, gc[h] - cum_row[h:h + 1, :], 0.0)), 0.0) for h in hs]
    q = [_l2n(conv[:, h * hd:(h + 1) * hd]) * hd ** -0.5 for h in hs]
    k = [_l2n(conv[:, gw + h * hd:gw + (h + 1) * hd]) for h in hs]
    bc = [beta[:, h:h + 1] for h in hs]
    kb = [k[h] * bc[h] for h in hs]
    kbf = [k[h].astype(BF16) for h in hs]
    m = [jnp.where(strict, _dot_nt(kb[h].astype(BF16), kbf[h]) * decay[h], 0.0) for h in hs]
    nmat = [-m[h] for h in hs]
    qm = m
    for _ in range(int(math.log2(ch)) - 1):
        qm = [_mm_bf16(qm[h], qm[h]) for h in hs]
        nmat = [nmat[h] + qm[h] + _mm_bf16(nmat[h], qm[h]) for h in hs]
    eg = [jnp.exp(gc[h]) for h in hs]
    rhs = [jnp.concatenate([conv[:, 2 * gw + h * hd:2 * gw + (h + 1) * hd] * bc[h], kb[h] * eg[h]], axis=1)
           for h in hs]
    uw = [rhs[h] + _mm_bf16(nmat[h], rhs[h]) for h in hs]
    attn = [(_dot_nt(q[h].astype(BF16), kbf[h]) * decay[h]).astype(BF16) for h in hs]
    qg = [(q[h] * eg[h]).astype(BF16) for h in hs]
    s = [s_ref[h] for h in hs]
    o_chunks = [[] for _ in hs]
    for c in range(tc // ch):
        r = slice(c * ch, (c + 1) * ch)
        sb = [s[h].astype(BF16) for h in hs]
        v_new = [uw[h][r, 0:hd] - jnp.dot(uw[h][r, hd:2 * hd].astype(BF16), sb[h], preferred_element_type=F32)
                 for h in hs]
        vb = [v_new[h].astype(BF16) for h in hs]
        for h in hs:
            o_chunks[h].append(jnp.dot(qg[h][r], sb[h], preferred_element_type=F32)
                               + jnp.dot(attn[h][r, r], vb[h], preferred_element_type=F32))
        g_last = [gc[h][(c + 1) * ch - 1:(c + 1) * ch, :] for h in hs]
        kd = [(k[h][r] * jnp.exp(g_last[h] - gc[h][r])).astype(BF16) for h in hs]
        s = [s[h] * jnp.exp(g_last[h]) + _dot_tn(kd[h], vb[h]) for h in hs]
    outs = []
    for h in hs:
        s_ref[h] = s[h]
        o = jnp.concatenate(o_chunks[h], axis=0)
        o = o * lax.rsqrt(jnp.mean(o * o, axis=-1, keepdims=True) + RMS_EPS) * ng_ref[:, h * hd:(h + 1) * hd]
        zh = z[:, h * hd:(h + 1) * hd]
        outs.append(o * (zh * jax.nn.sigmoid(zh)))
    o_ref[...] = jnp.concatenate(outs, axis=1)
    sfin_ref[0] = jnp.stack(s, axis=0)


def _gdn_prompt(proj, a_t, conv_buf8, s0, conv_w, a_log, dt_bias, norm_g, bsz, n, tc=256):
    hd, nh, gw = HEAD_DIM, N_HEADS, GROUP_WIDTH
    nb = n // tc
    return pl.pallas_call(
        functools.partial(_gdn_body, tc=tc),
        grid=(bsz, nb),
        in_specs=[pl.BlockSpec((tc, 3 * gw), lambda b, j: (b * nb + j, C_GQKV // (3 * gw))),
                  pl.BlockSpec((tc, gw), lambda b, j: (b * nb + j, C_GZ // gw)),
                  pl.BlockSpec((tc, 128), lambda b, j: (b * nb + j, C_SMALL // 128)),
                  pl.BlockSpec((1, 8, tc), lambda b, j: (b, 0, j)),
                  pl.BlockSpec((1, 8, 3 * gw), lambda b, j: (b, 0, 0)),
                  pl.BlockSpec((1, nh, hd, hd), lambda b, j: (b, 0, 0, 0)),
                  _resident((CONV_W, 3 * gw)), _resident((1, nh)), _resident((1, nh)), _resident((nh, 1)),
                  _resident((nh, 1)), _resident((1, gw))],
        out_specs=[pl.BlockSpec((tc, gw), lambda b, j: (b * nb + j, 0)),
                   pl.BlockSpec((1, nh, hd, hd), lambda b, j: (b, 0, 0, 0))],
        out_shape=[jax.ShapeDtypeStruct((bsz * n, gw), F32), jax.ShapeDtypeStruct((bsz, nh, hd, hd), F32)],
        scratch_shapes=[pltpu.VMEM((tc + 8, 3 * gw), F32), pltpu.VMEM((nh, hd, hd), F32)],
        compiler_params=_cparams(2),
        name="gdn_prompt",
    )(proj, proj, proj, a_t, conv_buf8, s0, conv_w, a_log.reshape(1, nh), dt_bias.reshape(1, nh),
      a_log.reshape(nh, 1), dt_bias.reshape(nh, 1), norm_g)


def _head_ones():
    h = np.arange(GROUP_WIDTH) // HEAD_DIM
    return jnp.asarray((h[:, None] == h[None, :]).astype(np.float32))


def _head_scores(k4, q, ones):
    return jnp.dot(k4 * q, ones, precision=HI, preferred_element_type=F32) * HEAD_DIM ** -0.5


def _tile4(x):
    return jnp.concatenate([x] * N_HEADS, axis=1)


def _fox_dec_body(pt_ref, q_ref, kvn_ref, ffn_ref, bf_ref, g_ref, *rest, npg, nstep):
    kv_refs, lf_refs = rest[0:npg], rest[npg:2 * npg]
    o_ref, lfo_ref, m_ref, l_ref, acc_ref, car_ref = rest[2 * npg:]
    j = pl.program_id(1)
    gw, hd, nh = GROUP_WIDTH, HEAD_DIM, N_HEADS
    scale = hd ** -0.5
    q = q_ref[0]
    npos = kv_refs[0].shape[3]

    def per_head(x):
        return jnp.sum(x.reshape(nh, hd, x.shape[1]), axis=1)

    def spread(x):
        return jnp.broadcast_to(x[:, None, :], (nh, hd, x.shape[1])).reshape(gw, x.shape[1])

    @pl.when(j == 0)
    def _():
        lf_new = jax.nn.log_sigmoid(ffn_ref[0] + bf_ref[...])
        lfo_ref[0] = lf_new
        m_ref[...] = jnp.full(m_ref.shape, NEG, F32)
        l_ref[...] = jnp.zeros(l_ref.shape, F32)
        acc_ref[...] = jnp.zeros(acc_ref.shape, F32)
        car_ref[...] = jnp.broadcast_to(lf_new, car_ref.shape)

    ri = lax.broadcasted_iota(jnp.int32, (npos, npos), 0)
    ci = lax.broadcasted_iota(jnp.int32, (npos, npos), 1)
    later = (ri > ci).astype(F32)
    m, l, acc, car = m_ref[...], l_ref[...], acc_ref[...], car_ref[...]
    lfs = [lf_refs[i][0, 0] for i in range(npg)]
    qk = [per_head(kv_refs[i][0, 0, 0:gw, :] * q) * scale for i in range(npg)]
    suf = [jnp.dot(lfs[i], later, precision=HI, preferred_element_type=F32) for i in range(npg)]
    tot = [jnp.sum(lfs[i], axis=1, keepdims=True) for i in range(npg)]
    for i in range(npg):
        s = qk[i] + car + suf[i]
        m_new = jnp.maximum(m, s)
        a = jnp.exp(m - m_new)
        p = jnp.exp(s - m_new)
        l = a * l + p
        acc = spread(a) * acc + spread(p) * kv_refs[i][0, 0, gw:2 * gw, :]
        m = m_new
        car = car + tot[i]
    m_ref[...], l_ref[...], acc_ref[...], car_ref[...] = m, l, acc, car

    @pl.when(j == nstep - 1)
    def _():
        kvn = kvn_ref[0]
        s_new = per_head(kvn[0:gw, :] * q) * scale
        m_tot = jnp.maximum(jnp.max(m, axis=1, keepdims=True), s_new)
        w = jnp.exp(m - m_tot)
        p_new = jnp.exp(s_new - m_tot)
        l_tot = jnp.sum(l * w, axis=1, keepdims=True) + p_new
        acc_tot = jnp.sum(acc * spread(w), axis=1, keepdims=True) + spread(p_new) * kvn[gw:2 * gw, :]
        o = acc_tot / spread(l_tot)
        o_ref[0] = o * lax.rsqrt(jnp.mean(o * o, axis=0, keepdims=True) + RMS_EPS) * g_ref[...]


def _fox_decode(q_col, kvn_col, ffn_col, page_table, kv_cache_t, lf_cache_t, layer, bf_col, g_col, npg=8):
    bsz = q_col.shape[0]
    n_pages = page_table.shape[1]
    gw, nh = GROUP_WIDTH, N_HEADS
    npos = kv_cache_t.shape[3]
    nstep = n_pages // npg
    page = lambda i: (lambda b, j, pt: (layer, pt[b, n_pages - 1 - (j * npg + i)], 0, 0))
    const2 = lambda b, j, pt: (0, 0)
    per_b = lambda b, j, pt: (b, 0, 0)
    grid_spec = pltpu.PrefetchScalarGridSpec(
        num_scalar_prefetch=1,
        grid=(bsz, nstep),
        in_specs=[pl.BlockSpec((1, gw, 1), per_b), pl.BlockSpec((1, 2 * gw, 1), per_b), pl.BlockSpec((1, nh, 1), per_b),
                  pl.BlockSpec((nh, 1), const2), pl.BlockSpec((gw, 1), const2)]
        + [pl.BlockSpec((1, 1, 2 * gw, npos), page(i)) for i in range(npg)]
        + [pl.BlockSpec((1, 1, nh, npos), page(i)) for i in range(npg)],
        out_specs=[pl.BlockSpec((1, gw, 1), per_b), pl.BlockSpec((1, nh, 1), per_b)],
        scratch_shapes=[pltpu.VMEM((nh, npos), F32), pltpu.VMEM((nh, npos), F32), pltpu.VMEM((gw, npos), F32),
                        pltpu.VMEM((nh, npos), F32)],
    )
    return pl.pallas_call(
        functools.partial(_fox_dec_body, npg=npg, nstep=nstep),
        grid_spec=grid_spec,
        out_shape=[jax.ShapeDtypeStruct((bsz, gw, 1), F32), jax.ShapeDtypeStruct((bsz, nh, 1), F32)],
        compiler_params=_cparams(2),
        name="fox_decode",
    )(page_table, q_col, kvn_col, ffn_col, bf_col, g_col, *([kv_cache_t] * npg), *([lf_cache_t] * npg))


def _nsa_dec_cmp_body(pt_ref, q_ref, pw_ref, slope_ref, ones_ref, *rest, npg, nstep, p0):
    pg_refs = rest[0:npg]
    oc_ref, idx_ref, kvc_ref, imp_ref = rest[npg:]
    j = pl.program_id(1)
    gw, hd, nh = GROUP_WIDTH, HEAD_DIM, N_HEADS
    per_page = 128 // CMP_BLOCK
    pw = pw_ref[...]
    pooled = [jnp.sum((pg_refs[i][0, 0].T * pw).reshape(per_page, CMP_BLOCK, 2 * hd), axis=1) for i in range(npg)]
    r0 = pl.multiple_of(j * (npg * per_page), npg * per_page)
    kvc_ref[pl.ds(r0, npg * per_page), :] = jnp.concatenate(pooled, axis=0)

    @pl.when(j == nstep - 1)
    def _():
        n_cmp = kvc_ref.shape[0]
        n_sel = n_cmp // 2
        q = q_ref[0]
        kvc = kvc_ref[...]
        cmp_end = (lax.broadcasted_iota(jnp.int32, (n_cmp, 1), 0) + 1) * CMP_BLOCK - 1
        dist = p0 - cmp_end
        s = _head_scores(_tile4(kvc[:, 0:hd]), q, ones_ref[...]) - slope_ref[...] * dist.astype(F32)
        ok = dist >= 0
        s = jnp.where(ok, s, NEG)
        p = jnp.where(ok, jnp.exp(s - jnp.max(s, axis=0, keepdims=True)), 0.0)
        p = p / jnp.maximum(jnp.sum(p, axis=0, keepdims=True), 1e-30)
        oc_ref[0] = jnp.sum(p * _tile4(kvc[:, hd:2 * hd]), axis=0, keepdims=True)
        imp = p[:, 0:1]
        for h in range(1, nh):
            imp = imp + p[:, h * hd:h * hd + 1]
        imp_ref[...] = jnp.broadcast_to(imp, imp_ref.shape)
        colm = imp_ref[pl.ds(0, n_sel, stride=2), :] + imp_ref[pl.ds(1, n_sel, stride=2), :]
        ri = lax.broadcasted_iota(jnp.int32, (n_sel, n_sel), 0)
        ci = lax.broadcasted_iota(jnp.int32, (n_sel, n_sel), 1)
        colm = jnp.where(ri == 0, FORCE_SCORE, colm)
        rowm = colm.T
        beats = (rowm > colm) | ((rowm == colm) & (ci < ri))
        rank = jnp.sum(beats.astype(F32), axis=1, keepdims=True)
        sel = (rank < SEL_TOPK - 1).astype(F32)
        pos = jnp.dot((ci < ri).astype(F32), jnp.broadcast_to(sel, (n_sel, n_sel)), preferred_element_type=F32)
        onehot = jnp.where((sel > 0.5) & (pos == ci.astype(F32)), ri.astype(F32), 0.0)
        idx_ref[0] = jnp.sum(onehot, axis=0, keepdims=True).astype(jnp.int32)


def _nsa_dec_cmp(proj3, page_table, cache2, layer, pw128, slope_e, p0, npg=16):
    bsz = proj3.shape[0]
    n_pages = page_table.shape[1]
    gw, hd = GROUP_WIDTH, HEAD_DIM
    nstep = n_pages // npg
    n_cmp = p0 // CMP_BLOCK
    assert p0 % SEL_BLOCK == 0 and n_cmp // 2 == 128 and p0 == n_pages * 128
    page = lambda i: (lambda b, j, pt: (layer, pt[b, j * npg + i], 0, 0))
    const2 = lambda b, j, pt: (0, 0)
    grid_spec = pltpu.PrefetchScalarGridSpec(
        num_scalar_prefetch=1,
        grid=(bsz, nstep),
        in_specs=[pl.BlockSpec((1, 1, gw), lambda b, j, pt: (b, 0, C_NQ // gw)),
                  pl.BlockSpec((128, 2 * hd), const2), pl.BlockSpec((1, gw), const2), pl.BlockSpec((gw, gw), const2)]
        + [pl.BlockSpec((1, 1, 2 * hd, cache2.shape[3]), page(i)) for i in range(npg)],
        out_specs=[pl.BlockSpec((1, 1, gw), lambda b, j, pt: (b, 0, 0)),
                   pl.BlockSpec((1, 1, 128), lambda b, j, pt: (b, 0, 0))],
        scratch_shapes=[pltpu.VMEM((n_cmp, 2 * hd), F32), pltpu.VMEM((n_cmp, 128), F32)],
    )
    return pl.pallas_call(
        functools.partial(_nsa_dec_cmp_body, npg=npg, nstep=nstep, p0=p0),
        grid_spec=grid_spec,
        out_shape=[jax.ShapeDtypeStruct((bsz, 1, gw), F32), jax.ShapeDtypeStruct((bsz, 1, 128), jnp.int32)],
        compiler_params=_cparams(2),
        name="nsa_dec_cmp",
    )(page_table, proj3, pw128, slope_e, _head_ones(), *([cache2] * npg))


def _nsa_dec_sel_body(pt_ref, idx_ref, q_ref, new_ref, wnew_ref, small_ref, oc_ref, win_ref, slope_ref, ones_ref,
                      eg_ref, g_ref, *rest, nsel, p0):
    blk_refs = rest[0:nsel]
    o_ref = rest[nsel]
    b = pl.program_id(0)
    hd = HEAD_DIM
    q = q_ref[0]
    ones = ones_ref[...]
    slope = slope_ref[...]
    new = new_ref[0]
    s_new = _head_scores(_tile4(new[:, 2 * hd:3 * hd]), q, ones)
    v_new = _tile4(new[:, 3 * hd:4 * hd])
    ss, vs = [], []
    r = lax.broadcasted_iota(jnp.int32, (SEL_BLOCK, 1), 0)
    for i in range(nsel):
        page_t = blk_refs[i][0, 0].T
        half = idx_ref[b, i] % (page_t.shape[0] // SEL_BLOCK)
        blk = page_t[0:SEL_BLOCK]
        for hh in range(1, page_t.shape[0] // SEL_BLOCK):
            blk = jnp.where(half == hh, page_t[hh * SEL_BLOCK:(hh + 1) * SEL_BLOCK], blk)
        dist = (p0 - idx_ref[b, i] * SEL_BLOCK) - r
        ss.append(_head_scores(_tile4(blk[:, 0:hd]), q, ones) - slope * dist.astype(F32))
        vs.append(_tile4(blk[:, hd:2 * hd]))
    s = jnp.concatenate(ss, axis=0)
    v = jnp.concatenate(vs, axis=0)
    m = jnp.maximum(jnp.max(s, axis=0, keepdims=True), s_new)
    p, p_new = jnp.exp(s - m), jnp.exp(s_new - m)
    o_s = (jnp.sum(p * v, axis=0, keepdims=True) + p_new * v_new) / (jnp.sum(p, axis=0, keepdims=True) + p_new)
    win = win_ref[0, 0].T
    wn = wnew_ref[0]
    nw = win.shape[0]
    dist_w = nw - lax.broadcasted_iota(jnp.int32, (nw, 1), 0)
    ok = dist_w < WINDOW
    s_w = jnp.where(ok, _head_scores(_tile4(win[:, 0:hd]), q, ones) - slope * dist_w.astype(F32), NEG)
    sw_new = _head_scores(_tile4(wn[:, 0:hd]), q, ones)
    mw = jnp.maximum(jnp.max(s_w, axis=0, keepdims=True), sw_new)
    pw_, pw_new = jnp.where(ok, jnp.exp(s_w - mw), 0.0), jnp.exp(sw_new - mw)
    o_w = ((jnp.sum(pw_ * _tile4(win[:, hd:2 * hd]), axis=0, keepdims=True) + pw_new * _tile4(wn[:, hd:2 * hd]))
           / (jnp.sum(pw_, axis=0, keepdims=True) + pw_new))
    gt = jax.nn.sigmoid(small_ref[0])
    ge = [jnp.dot(gt, eg_ref[c], precision=HI, preferred_element_type=F32) for c in range(3)]
    o_ref[0] = _rms(ge[0] * oc_ref[0] + ge[1] * o_s + ge[2] * o_w, g_ref[...])


def _nsa_dec_sel(proj3, page_table, idx, o_c, cache, win_state, layer, slope_e, g, p0):
    bsz = proj3.shape[0]
    gw, hd, nh = GROUP_WIDTH, HEAD_DIM, N_HEADS
    nsel = SEL_TOPK - 1
    nw = win_state.shape[3]
    eg = np.zeros((3, 128, gw), np.float32)
    for c in range(3):
        for h in range(nh):
            eg[c, S_NGATE + 3 * h + c, h * hd:(h + 1) * hd] = 1.0
    n_pool = cache.shape[1]
    per_page = cache.shape[3] // SEL_BLOCK

    def blk(i):
        def index_map(b, pt, ix):
            sel = jnp.clip(ix[b, i], 0, p0 // SEL_BLOCK - 1)
            page = jnp.clip(pt[b, sel // per_page], 0, n_pool - 1)
            return (layer, page, 1, 0)
        return index_map
    const2 = lambda b, pt, ix: (0, 0)
    grid_spec = pltpu.PrefetchScalarGridSpec(
        num_scalar_prefetch=2,
        grid=(bsz,),
        in_specs=[pl.BlockSpec((1, 1, gw), lambda b, pt, ix: (b, 0, C_NQ // gw)),
                  pl.BlockSpec((1, 1, gw), lambda b, pt, ix: (b, 0, C_NROWS // gw)),
                  pl.BlockSpec((1, 1, 2 * hd), lambda b, pt, ix: (b, 0, C_WIN // (2 * hd))),
                  pl.BlockSpec((1, 1, 128), lambda b, pt, ix: (b, 0, C_SMALL // 128)),
                  pl.BlockSpec((1, 1, gw), lambda b, pt, ix: (b, 0, 0)),
                  pl.BlockSpec((1, 1, 2 * hd, nw), lambda b, pt, ix: (layer, b, 0, 0)),
                  pl.BlockSpec((1, gw), const2), pl.BlockSpec((gw, gw), const2),
                  pl.BlockSpec((3, 128, gw), lambda b, pt, ix: (0, 0, 0)), pl.BlockSpec((1, gw), const2)]
        + [pl.BlockSpec((1, 1, 2 * hd, cache.shape[3]), blk(i)) for i in range(nsel)],
        out_specs=pl.BlockSpec((1, 1, gw), lambda b, pt, ix: (b, 0, 0)),
    )
    return pl.pallas_call(
        functools.partial(_nsa_dec_sel_body, nsel=nsel, p0=p0),
        grid_spec=grid_spec,
        out_shape=jax.ShapeDtypeStruct((bsz, 1, gw), F32),
        compiler_params=_cparams(1),
        name="nsa_dec_sel",
    )(page_table, idx, proj3, proj3, proj3, proj3, o_c, win_state, slope_e, _head_ones(), jnp.asarray(eg), g,
      *([cache] * nsel))


def _gdn_dec_body(qkv_ref, z_ref, small_ref, cb_ref, s0_ref, cw_ref, al_ref, dt_ref, ng_ref, o_ref, s_ref):
    hd, nh, gw = HEAD_DIM, N_HEADS, GROUP_WIDTH
    cb = cb_ref[0, 0]
    conv = cb[0:1] * cw_ref[0:1, :]
    for t in range(1, CONV_W - 1):
        conv = conv + cb[t:t + 1] * cw_ref[t:t + 1, :]
    conv = conv + qkv_ref[0] * cw_ref[CONV_W - 1:CONV_W, :]
    conv = conv * jax.nn.sigmoid(conv)
    small = small_ref[0]
    g = -jnp.exp(al_ref[...]) * jax.nn.softplus(small[:, S_GA:S_GA + nh] + dt_ref[...])
    beta = jax.nn.sigmoid(small[:, S_GB:S_GB + nh])
    z = z_ref[0]
    ri = lax.broadcasted_iota(jnp.int32, (hd, hd), 0)
    ci = lax.broadcasted_iota(jnp.int32, (hd, hd), 1)
    outs = []
    for h in range(nh):
        q = _l2n(conv[:, h * hd:(h + 1) * hd]) * hd ** -0.5
        k = _l2n(conv[:, gw + h * hd:gw + (h + 1) * hd])
        v = conv[:, 2 * gw + h * hd:2 * gw + (h + 1) * hd]
        eg = jnp.exp(g[:, h:h + 1])
        bc = beta[:, h:h + 1]
        s = s0_ref[0, 0, h]
        v_new = v * bc - jnp.dot(k * bc * eg, s, precision=HI, preferred_element_type=F32)
        o = (jnp.dot(q * eg, s, precision=HI, preferred_element_type=F32)
             + jnp.sum(q * k, axis=-1, keepdims=True) * v_new)
        k_col = jnp.sum(jnp.where(ri == ci, jnp.broadcast_to(k, (hd, hd)), 0.0), axis=1, keepdims=True)
        s_ref[0, h] = s * eg + k_col * v_new
        o = o * lax.rsqrt(jnp.mean(o * o, axis=-1, keepdims=True) + RMS_EPS) * ng_ref[:, h * hd:(h + 1) * hd]
        zh = z[:, h * hd:(h + 1) * hd]
        outs.append(o * (zh * jax.nn.sigmoid(zh)))
    o_ref[0] = jnp.concatenate(outs, axis=1)


def _gdn_decode(proj3, conv_buf, s0, layer, conv_w, a_log, dt_bias, norm_g):
    bsz = proj3.shape[0]
    hd, nh, gw = HEAD_DIM, N_HEADS, GROUP_WIDTH
    const2 = lambda b: (0, 0)
    return pl.pallas_call(
        _gdn_dec_body,
        grid=(bsz,),
        in_specs=[pl.BlockSpec((1, 1, 3 * gw), lambda b: (b, 0, C_GQKV // (3 * gw))),
                  pl.BlockSpec((1, 1, gw), lambda b: (b, 0, C_GZ // gw)),
                  pl.BlockSpec((1, 1, 128), lambda b: (b, 0, C_SMALL // 128)),
                  pl.BlockSpec((1, 1, CONV_W - 1, 3 * gw), lambda b: (layer, b, 0, 0)),
                  pl.BlockSpec((1, 1, nh, hd, hd), lambda b: (layer, b, 0, 0, 0)),
                  pl.BlockSpec((CONV_W, 3 * gw), const2), pl.BlockSpec((1, nh), const2), pl.BlockSpec((1, nh), const2),
                  pl.BlockSpec((1, gw), const2)],
        out_specs=[pl.BlockSpec((1, 1, gw), lambda b: (b, 0, 0)), pl.BlockSpec((1, nh, hd, hd), lambda b: (b, 0, 0, 0))],
        out_shape=[jax.ShapeDtypeStruct((bsz, 1, gw), F32), jax.ShapeDtypeStruct((bsz, nh, hd, hd), F32)],
        compiler_params=_cparams(1),
        name="gdn_decode",
    )(proj3, proj3, proj3, conv_buf, s0, conv_w, a_log.reshape(1, nh), dt_bias.reshape(1, nh), norm_g)


def kernel(x_prompt, x_sample, cache_nsa_kv, cache_fox_kv, cache_fox_logf, state_nsa_win, state_gdn, state_gdn_conv, state_s5_re, state_s5_im, page_table, w_in, nsa_pool, s5_a_re, s5_a_im, s5_b_re, s5_b_im, s5_c_re, s5_c_im, s5_d, s5_log_dt, s5_w_glu, gdn_conv, gdn_a_log, gdn_dt_bias, fox_b_f, mix_norm, w_out, ln1_g, ln1_b, ln2_g, ln2_b, router_w, router_b, exp_w1, exp_w3, exp_w2):
    B, L, D = x_prompt.shape
    BS = x_sample.shape[0]
    hd, nh, gw = HEAD_DIM, N_HEADS, GROUP_WIDTH
    ns = S5_GROUPS * S5_STATE
    n_pool = cache_nsa_kv.shape[1]
    p0 = page_table.shape[1] * cache_nsa_kv.shape[2]
    slopes = 2.0 ** (-8.0 * (jnp.arange(nh, dtype=F32) + 1.0) / nh)
    slope_e = jnp.repeat(slopes, hd).reshape(1, gw)
    xp = x_prompt.reshape(B * L, D)
    xs = x_sample.reshape(BS, D)
    acc_p = [[] for _ in range(8)]
    acc_s = [[] for _ in range(8)]
    nsa_cache = cache_nsa_kv.transpose(0, 1, 3, 4, 2).reshape(DEPTH, n_pool, 4 * hd, -1)
    fox_cache_t = cache_fox_kv.transpose(0, 1, 3, 4, 5, 2).reshape(DEPTH, n_pool, 2 * gw, -1)
    fox_logf_t = cache_fox_logf.transpose(0, 1, 3, 2)
    win_all = state_nsa_win.transpose(0, 1, 3, 4, 2).reshape(DEPTH, BS, 2 * hd, -1)
    for l in range(DEPTH):
        wp_f = _permute_w_in(w_in[l])
        wo_f = w_out[l]
        wp, wo = wp_f.astype(BF16), wo_f.astype(BF16)
        w1 = exp_w1[l].astype(BF16).transpose(1, 0, 2).reshape(D, -1)
        w3 = exp_w3[l].astype(BF16).transpose(1, 0, 2).reshape(D, -1)
        w2 = exp_w2[l].astype(BF16).reshape(-1, D)
        rb = router_b.reshape(1, -1)
        ln1 = (ln1_g[l].reshape(1, D), ln1_b[l].reshape(1, D))
        ln2 = (ln2_g[l].reshape(1, D), ln2_b[l].reshape(1, D))
        g_nsa, g_s5, g_gdn, g_fox = [g.reshape(1, gw) for g in jnp.split(mix_norm[l], N_MIXERS)]
        s5_args = (s5_a_re[l], s5_a_im[l], s5_b_re[l], s5_b_im[l], s5_c_re[l], s5_c_im[l], s5_d[l],
                   s5_log_dt[l], s5_w_glu[l])
        prm, prm_f = _s5_params(*s5_args, BF16), _s5_params(*s5_args, F32)

        proj = _in_proj(xp, wp, 256)
        o_nsa = _nsa_prompt(proj, slopes, nsa_pool[l], g_nsa, B, L)
        small = proj[:, C_SMALL:C_SMALL + 128].reshape(B, L, 128)
        logf_t, cum_t = _fox_prep(small[:, :, S_FF:S_FF + nh].transpose(0, 2, 1), fox_b_f[l].reshape(nh, 1))
        o_fox = _fox_prompt(proj, cum_t, g_fox, B, L, 256)
        u_tm = proj[:, C_S5U:C_S5U + gw].reshape(B, L, gw).transpose(1, 0, 2).reshape(L * B, gw)
        zst = jnp.zeros((B, ns), F32)
        o_s5, s5r, s5i = _s5(u_tm, prm, zst, zst, g_s5, B, L, 256)
        o_s5 = o_s5.reshape(L, B, gw).transpose(1, 0, 2).reshape(B * L, gw)
        a_t = jnp.pad(small[:, :, S_GA:S_GA + nh].transpose(0, 2, 1), ((0, 0), (0, 8 - nh), (0, 0)))
        o_gdn, gdn_st = _gdn_prompt(proj, a_t, jnp.zeros((B, 8, 3 * gw), F32), jnp.zeros((B, nh, hd, hd), F32),
                                    gdn_conv[l], gdn_a_log[l], gdn_dt_bias[l], g_gdn, B, L)
        xp = _out_proj_ln(xp, [o_nsa, o_s5, o_gdn, o_fox], wo, *ln1, 256)
        xp = _moe_ln(xp, router_w, rb, w1, w3, w2, *ln2, 256)
        st = (proj[:, C_NROWS:C_NROWS + gw].reshape(B, L, 4, hd),
              proj[:, C_FKV:C_FKV + 2 * gw].reshape(B, L, 2, nh, hd),
              logf_t.transpose(0, 2, 1),
              proj[:, C_WIN:C_WIN + 2 * hd].reshape(B, L, 2, hd)[:, L - min(WINDOW, L):],
              gdn_st,
              proj[:, C_GQKV:C_GQKV + 3 * gw].reshape(B, L, 3 * gw)[:, L - (CONV_W - 1):],
              s5r.reshape(B, S5_GROUPS, S5_STATE), s5i.reshape(B, S5_GROUPS, S5_STATE))
        for a, v in zip(acc_p, st):
            a.append(v)

        proj_s = _in_proj(xs, wp_f, BS)
        proj3 = proj_s.reshape(BS, 1, PROJ_COLS)
        pw = jnp.concatenate([jnp.broadcast_to(nsa_pool[l][0][:, None], (CMP_BLOCK, hd)),
                              jnp.broadcast_to(nsa_pool[l][1][:, None], (CMP_BLOCK, hd))], axis=1)
        o_c, idx = _nsa_dec_cmp(proj3, page_table, nsa_cache, l, jnp.tile(pw, (128 // CMP_BLOCK, 1)), slope_e, p0)
        win_state = state_nsa_win[l]
        o_nsa = _nsa_dec_sel(proj3, page_table, idx.reshape(BS, 128), o_c, nsa_cache, win_all, l, slope_e, g_nsa, p0)
        ff_s = proj_s[:, C_SMALL + S_FF:C_SMALL + S_FF + nh]
        o_fox, lfo = _fox_decode(proj_s[:, C_FQ:C_FQ + gw].reshape(BS, gw, 1),
                                 proj_s[:, C_FKV:C_FKV + 2 * gw].reshape(BS, 2 * gw, 1), ff_s.reshape(BS, nh, 1),
                                 page_table, fox_cache_t, fox_logf_t, l, fox_b_f[l].reshape(nh, 1),
                                 g_fox.reshape(gw, 1))
        o_s5, s5r, s5i = _s5(proj_s[:, C_S5U:C_S5U + gw], prm_f, state_s5_re[l].reshape(BS, ns),
                             state_s5_im[l].reshape(BS, ns), g_s5, BS, 1, 1)
        o_gdn, gdn_st = _gdn_decode(proj3, state_gdn_conv, state_gdn, l, gdn_conv[l], gdn_a_log[l],
                                    gdn_dt_bias[l], g_gdn)
        xs = _out_proj_ln(xs, [o_nsa.reshape(BS, gw), o_s5, o_gdn.reshape(BS, gw), o_fox.reshape(BS, gw)], wo_f, *ln1, BS)
        xs = _moe_dec(xs, router_w, rb, exp_w1, exp_w3, exp_w2, l, *ln2)
        st = (proj_s[:, C_NROWS:C_NROWS + gw].reshape(BS, 1, 4, hd),
              proj_s[:, C_FKV:C_FKV + 2 * gw].reshape(BS, 1, 2, nh, hd),
              lfo.reshape(BS, 1, nh),
              jnp.concatenate([win_state[:, 1:], proj_s[:, C_WIN:C_WIN + 2 * hd].reshape(BS, 1, 2, hd)], axis=1),
              gdn_st,
              jnp.concatenate([state_gdn_conv[l][:, 1:], proj_s[:, C_GQKV:C_GQKV + 3 * gw].reshape(BS, 1, 3 * gw)], axis=1),
              s5r.reshape(BS, S5_GROUPS, S5_STATE), s5i.reshape(BS, S5_GROUPS, S5_STATE))
        for a, v in zip(acc_s, st):
            a.append(v)
    nsa_rows_p, fox_kv_p, fox_logf_p, nsa_win_p, gdn_p, gdn_conv_p, s5_re_p, s5_im_p = [jnp.stack(a) for a in acc_p]
    nsa_rows_s, fox_kv_s, fox_logf_s, nsa_win_s, gdn_s, gdn_conv_s, s5_re_s, s5_im_s = [jnp.stack(a) for a in acc_s]
    return (xp.reshape(B, L, D), xs.reshape(BS, 1, D), nsa_rows_p, nsa_rows_s, fox_kv_p, fox_kv_s, fox_logf_p, fox_logf_s,
            nsa_win_p, nsa_win_s, gdn_p, gdn_s, gdn_conv_p, gdn_conv_s, s5_re_p, s5_re_s, s5_im_p, s5_im_s)
```

```python
import functools
import math

import numpy as np
import jax
import jax.numpy as jnp
from jax import lax
from jax.experimental import pallas as pl
from jax.experimental.pallas import tpu as pltpu

F32 = jnp.float32
BF16 = jnp.bfloat16
HI = lax.Precision.HIGHEST

DEPTH = 4
N_MIXERS = 4
GROUP_WIDTH = 256
HEAD_DIM = 64
N_HEADS = 4
CMP_BLOCK = 32
SEL_BLOCK = 64
SEL_TOPK = 16
WINDOW = 512
FORCE_SCORE = 1.0e4
S5_GROUP = 16
S5_GROUPS = 16
S5_STATE = 64
CONV_W = 4
GDN_CHUNK = 64
N_EXPERTS = 16
N_EXPERT_GROUPS = 4
EXPERTS_PER_GROUP = 4
D_EXPERT = 256
ALPHA = (2.0 * DEPTH) ** 0.25
LN_EPS = 1e-5
RMS_EPS = 1e-6
NEG = -1e30

VMEM_LIMIT = 56 * 1024 * 1024

C_GQKV, C_NQ, C_FKV, C_NROWS, C_S5U, C_GZ, C_FQ, C_WIN, C_SMALL = 0, 768, 1024, 1536, 1792, 2048, 2304, 2560, 2688
PROJ_COLS = 2816
S_NGATE, S_GA, S_GB, S_FF = 0, 12, 16, 20


_IN_PROJ_SEGS = ((908, 1676), (0, 256), (2196, 2708), (256, 512), (652, 908), (1684, 1940), (1940, 2196),
                 (512, 640), (640, 652), (1676, 1680), (1680, 1684), (2708, 2712))


def _in_proj_perm():
    return np.concatenate([np.arange(a, b) for a, b in _IN_PROJ_SEGS])


def _permute_w_in(w):
    used = sum(b - a for a, b in _IN_PROJ_SEGS)
    parts = [w[:, a:b] for a, b in _IN_PROJ_SEGS] + [jnp.zeros((w.shape[0], PROJ_COLS - used), w.dtype)]
    return jnp.concatenate(parts, axis=1)


def _dotw(a, w):
    if w.dtype == BF16:
        return jnp.dot(a.astype(BF16), w, preferred_element_type=F32)
    return jnp.dot(a, w, precision=HI, preferred_element_type=F32)


def _cparams(n_axes):
    return pltpu.CompilerParams(dimension_semantics=("arbitrary",) * n_axes, vmem_limit_bytes=VMEM_LIMIT)


def _resident(shape):
    nd = len(shape)
    return pl.BlockSpec(shape, lambda *_: (0,) * nd, pipeline_mode=pl.Buffered(1))


def _dot_nt(a, b):
    return lax.dot_general(a, b, (((1,), (1,)), ((), ())), preferred_element_type=F32)


def _dot_tn(a, b):
    return lax.dot_general(a, b, (((0,), (0,)), ((), ())), preferred_element_type=F32)


def _rms(x, g):
    return x * lax.rsqrt(jnp.mean(x * x, axis=-1, keepdims=True) + RMS_EPS) * g


def _ln(x, g, b):
    mu = jnp.mean(x, axis=-1, keepdims=True)
    xc = x - mu
    var = jnp.mean(xc * xc, axis=-1, keepdims=True)
    return xc * lax.rsqrt(var + LN_EPS) * g + b


def _in_proj_body(x_ref, w_ref, o_ref, *, nch):
    x = x_ref[...]
    x = x.astype(BF16) if w_ref.dtype == BF16 else x
    for j in range(0, o_ref.shape[1], nch):
        o_ref[:, j:j + nch] = _dotw(x, w_ref[:, j:j + nch])


def _in_proj(x2d, w, tm):
    m, k = x2d.shape
    n = w.shape[1]
    return pl.pallas_call(
        functools.partial(_in_proj_body, nch=256),
        grid=(m // tm,),
        in_specs=[pl.BlockSpec((tm, k), lambda i: (i, 0)), _resident((k, n))],
        out_specs=pl.BlockSpec((tm, n), lambda i: (i, 0)),
        out_shape=jax.ShapeDtypeStruct((m, n), F32),
        compiler_params=_cparams(1),
        name="in_proj",
    )(x2d, w)


def _out_proj_body(x_ref, o0_ref, o1_ref, o2_ref, o3_ref, w_ref, g_ref, b_ref, out_ref):
    gw = GROUP_WIDTH
    y = _dotw(o0_ref[...], w_ref[0:gw, :])
    y = y + _dotw(o1_ref[...], w_ref[gw:2 * gw, :])
    y = y + _dotw(o2_ref[...], w_ref[2 * gw:3 * gw, :])
    y = y + _dotw(o3_ref[...], w_ref[3 * gw:4 * gw, :])
    out_ref[...] = _ln(ALPHA * x_ref[...] + y, g_ref[...], b_ref[...])


def _out_proj_ln(x2d, mixers, w_out, g, b, tm):
    m, d = x2d.shape
    gw = GROUP_WIDTH
    row = lambda i: (i, 0)
    return pl.pallas_call(
        _out_proj_body,
        grid=(m // tm,),
        in_specs=[pl.BlockSpec((tm, d), row)] + [pl.BlockSpec((tm, gw), row)] * 4
        + [_resident((d, d)), _resident((1, d)), _resident((1, d))],
        out_specs=pl.BlockSpec((tm, d), row),
        out_shape=jax.ShapeDtypeStruct((m, d), F32),
        compiler_params=_cparams(1),
        name="out_proj_ln",
    )(x2d, *mixers, w_out, g, b)


def _moe_gate_t(lt):
    m = jnp.max(lt, axis=0, keepdims=True)
    p = jnp.exp(lt - m)
    probs = p / jnp.sum(p, axis=0, keepdims=True)
    rows = [probs[e:e + 1, :] for e in range(N_EXPERTS)]
    n = EXPERTS_PER_GROUP
    scores = []
    for g in range(N_EXPERT_GROUPS):
        r = rows[g * n:(g + 1) * n]
        best = None
        for i in range(n):
            for j in range(i + 1, n):
                s = r[i] + r[j]
                best = s if best is None else jnp.maximum(best, s)
        scores.append(best)
    grp = jnp.zeros_like(scores[0], dtype=jnp.int32)
    top = scores[0]
    for g in range(1, N_EXPERT_GROUPS):
        take = scores[g] > top
        grp = jnp.where(take, g, grp)
        top = jnp.where(take, scores[g], top)
    vals = []
    for j in range(n):
        v = rows[j]
        for g in range(1, N_EXPERT_GROUPS):
            v = jnp.where(grp == g, rows[g * n + j], v)
        vals.append(v)

    def first_argmax(vs):
        idx = jnp.zeros_like(grp)
        best = vs[0]
        for j in range(1, n):
            take = vs[j] > best
            idx = jnp.where(take, j, idx)
            best = jnp.where(take, vs[j], best)
        return best, idx

    v1, i1 = first_argmax(vals)
    v2, i2 = first_argmax([jnp.where(i1 == j, -jnp.inf, vals[j]) for j in range(n)])
    tot = v1 + v2
    w1, w2 = v1 / tot, v2 / tot
    e1, e2 = grp * n + i1, grp * n + i2
    gate = [jnp.where(e1 == e, w1, 0.0) + jnp.where(e2 == e, w2, 0.0) for e in range(N_EXPERTS)]
    return jnp.concatenate(gate, axis=0)


def _moe_body(x_ref, rw_ref, rb_ref, w1_ref, w3_ref, w2_ref, g_ref, b_ref, out_ref, *, epc):
    x = x_ref[...]
    tm = x.shape[0]
    logits = jnp.dot(x, rw_ref[...], precision=HI, preferred_element_type=F32) + rb_ref[...]
    gate = _moe_gate_t(logits.T).T
    xb = x.astype(BF16)
    cw = epc * D_EXPERT
    y = jnp.zeros((tm, x.shape[1]), F32)
    for c in range(N_EXPERTS // epc):
        a = jnp.dot(xb, w1_ref[:, c * cw:(c + 1) * cw], preferred_element_type=F32)
        b = jnp.dot(xb, w3_ref[:, c * cw:(c + 1) * cw], preferred_element_type=F32)
        ge = jnp.concatenate(
            [jnp.broadcast_to(gate[:, e:e + 1], (tm, D_EXPERT)) for e in range(c * epc, (c + 1) * epc)], axis=1)
        h = (a * jax.nn.sigmoid(a)) * b * ge
        y = y + jnp.dot(h.astype(BF16), w2_ref[c * cw:(c + 1) * cw, :], preferred_element_type=F32)
    out_ref[...] = _ln(ALPHA * x + y, g_ref[...], b_ref[...])


def _moe_ln(x2d, rw, rb, w1, w3, w2, g, b, tm):
    m, d = x2d.shape
    ne = N_EXPERTS * D_EXPERT
    row = lambda i: (i, 0)
    return pl.pallas_call(
        functools.partial(_moe_body, epc=4),
        grid=(m // tm,),
        in_specs=[pl.BlockSpec((tm, d), row), _resident((d, N_EXPERTS)), _resident((1, N_EXPERTS)),
                  _resident((d, ne)), _resident((d, ne)), _resident((ne, d)), _resident((1, d)), _resident((1, d))],
        out_specs=pl.BlockSpec((tm, d), row),
        out_shape=jax.ShapeDtypeStruct((m, d), F32),
        compiler_params=_cparams(1),
        name="moe_ln",
    )(x2d, rw, rb, w1, w3, w2, g, b)


def _moe_dec_body(x_ref, rw_ref, rb_ref, w1_ref, w3_ref, w2_ref, g_ref, b_ref, out_ref, ge_ref, acc_ref, *, epc):
    c = pl.program_id(0)
    x = x_ref[...]
    tm = x.shape[0]

    @pl.when(c == 0)
    def _():
        logits = jnp.dot(x, rw_ref[...], precision=HI, preferred_element_type=F32) + rb_ref[...]
        gate = _moe_gate_t(logits.T).T
        for cc in range(N_EXPERTS // epc):
            ge_ref[cc] = jnp.concatenate(
                [jnp.broadcast_to(gate[:, e:e + 1], (tm, D_EXPERT)) for e in range(cc * epc, (cc + 1) * epc)], axis=1)
        acc_ref[...] = jnp.zeros(acc_ref.shape, F32)

    ge = ge_ref[c]
    y = acc_ref[...]
    for e in range(epc):
        a = _dotw(x, w1_ref[0, e])
        b = _dotw(x, w3_ref[0, e])
        h = (a * jax.nn.sigmoid(a)) * b * ge[:, e * D_EXPERT:(e + 1) * D_EXPERT]
        y = y + _dotw(h, w2_ref[0, e])
    acc_ref[...] = y

    @pl.when(c == pl.num_programs(0) - 1)
    def _():
        out_ref[...] = _ln(ALPHA * x + acc_ref[...], g_ref[...], b_ref[...])


def _moe_dec(x2d, rw, rb, w1, w3, w2, layer, g, b, epc=2):
    m, d = x2d.shape
    cw = epc * D_EXPERT
    nchunk = N_EXPERTS // epc
    const = lambda c: (0, 0)
    return pl.pallas_call(
        functools.partial(_moe_dec_body, epc=epc),
        grid=(nchunk,),
        in_specs=[pl.BlockSpec((m, d), const), pl.BlockSpec((d, N_EXPERTS), const), pl.BlockSpec((1, N_EXPERTS), const),
                  pl.BlockSpec((1, epc, d, D_EXPERT), lambda c: (layer, c, 0, 0)),
                  pl.BlockSpec((1, epc, d, D_EXPERT), lambda c: (layer, c, 0, 0)),
                  pl.BlockSpec((1, epc, D_EXPERT, d), lambda c: (layer, c, 0, 0)), pl.BlockSpec((1, d), const),
                  pl.BlockSpec((1, d), const)],
        out_specs=pl.BlockSpec((m, d), const),
        out_shape=jax.ShapeDtypeStruct((m, d), F32),
        scratch_shapes=[pltpu.VMEM((nchunk, m, cw), F32), pltpu.VMEM((m, d), F32)],
        compiler_params=_cparams(1),
        name="moe_dec",
    )(x2d, rw, rb, w1, w3, w2, g, b)


def _fox_prep_body(ff_ref, bf_ref, logf_ref, cum_ref):
    lf = jax.nn.log_sigmoid(ff_ref[0] + bf_ref[...])
    logf_ref[0] = lf
    n = lf.shape[1]
    lane = lax.broadcasted_iota(jnp.int32, lf.shape, 1)
    c = lf
    s = 1
    while s < n:
        c = c + jnp.where(lane >= s, pltpu.roll(c, s, 1), 0.0)
        s *= 2
    cum_ref[0] = c


def _fox_prep(ff_t, bf):
    bsz, nh, n = ff_t.shape
    blk = pl.BlockSpec((1, nh, n), lambda b: (b, 0, 0))
    return pl.pallas_call(
        _fox_prep_body,
        grid=(bsz,),
        in_specs=[blk, _resident((nh, 1))],
        out_specs=[blk, blk],
        out_shape=[jax.ShapeDtypeStruct((bsz, nh, n), F32)] * 2,
        compiler_params=_cparams(1),
        name="fox_prep",
    )(ff_t, bf)


def _fox_body(q_ref, kv_ref, cq_ref, ck_ref, g_ref, o_ref, m_ref, l_ref, acc_ref, *, tq, tk):
    qi = pl.program_id(1)
    hd, nh, gw = HEAD_DIM, N_HEADS, GROUP_WIDTH
    scale = hd ** -0.5
    q = q_ref[...] * scale
    hs = range(nh)
    qb = [q[:, h * hd:(h + 1) * hd].astype(BF16) for h in hs]
    cq = cq_ref[0]
    cqs = [cq[:, h:h + 1] for h in hs]
    m_ref[...] = jnp.full(m_ref.shape, NEG, F32)
    l_ref[...] = jnp.zeros(l_ref.shape, F32)
    acc_ref[...] = jnp.zeros(acc_ref.shape, F32)
    q0 = qi * tq
    t_q = q0 + lax.broadcasted_iota(jnp.int32, (tq, tk), 0)
    lane_k = lax.broadcasted_iota(jnp.int32, (tq, tk), 1)

    def chunk(kc, last):
        r0 = pl.multiple_of(kc * tk, tk)
        kv = kv_ref[pl.ds(r0, tk), :]
        ck = ck_ref[0, :, pl.ds(r0, tk)]
        m_old = [m_ref[h] for h in hs]
        l_old = [l_ref[h] for h in hs]
        acc_old = [acc_ref[h] for h in hs]
        s = [_dot_nt(qb[h], kv[:, h * hd:(h + 1) * hd].astype(BF16)) + cqs[h] - ck[h:h + 1, :] for h in hs]
        if last:
            bias = jnp.where(r0 + lane_k <= t_q, 0.0, NEG)
            s = [x + bias for x in s]
        m_new = [jnp.maximum(m_old[h], jnp.max(s[h], axis=1, keepdims=True)) for h in hs]
        a = [jnp.exp(m_old[h] - m_new[h]) for h in hs]
        p = [jnp.exp(s[h] - m_new[h]) for h in hs]
        l_new = [a[h] * l_old[h] + jnp.sum(p[h], axis=1, keepdims=True) for h in hs]
        pv = [jnp.dot(p[h].astype(BF16), kv[:, gw + h * hd:gw + (h + 1) * hd].astype(BF16),
                      preferred_element_type=F32) for h in hs]
        for h in hs:
            m_ref[h] = m_new[h]
            l_ref[h] = l_new[h]
            acc_ref[h] = a[h] * acc_old[h] + pv[h]

    def body(kc, carry):
        chunk(kc, False)
        return carry

    n_full = q0 // tk
    lax.fori_loop(0, n_full, body, 0)
    chunk(n_full, True)
    o_ref[...] = _rms(jnp.concatenate([acc_ref[h] / l_ref[h] for h in hs], axis=1), g_ref[...])


def _fox_prompt(proj, cum_t, g, bsz, n, tq, tk=512):
    nh, gw = N_HEADS, GROUP_WIDTH
    nq = n // tq
    assert tk % tq == 0 and n % tk == 0
    return pl.pallas_call(
        functools.partial(_fox_body, tq=tq, tk=tk),
        grid=(bsz, nq),
        in_specs=[pl.BlockSpec((tq, gw), lambda b, i: (b * nq + i, C_FQ // gw)),
                  pl.BlockSpec((n, 2 * gw), lambda b, i: (b, C_FKV // (2 * gw))),
                  pl.BlockSpec((1, tq, nh), lambda b, i: (b, i, 0)),
                  pl.BlockSpec((1, nh, n), lambda b, i: (b, 0, 0)),
                  _resident((1, gw))],
        out_specs=pl.BlockSpec((tq, gw), lambda b, i: (b * nq + i, 0)),
        out_shape=jax.ShapeDtypeStruct((bsz * n, gw), F32),
        scratch_shapes=[pltpu.VMEM((nh, tq, 1), F32), pltpu.VMEM((nh, tq, 1), F32),
                        pltpu.VMEM((nh, tq, HEAD_DIM), F32)],
        compiler_params=_cparams(2),
        name="fox_prompt",
    )(proj, proj, cum_t.transpose(0, 2, 1), cum_t, g)


def _slope_col(slopes_ref, tq):
    hrow = lax.broadcasted_iota(jnp.int32, (N_HEADS * tq, 1), 0) // tq
    s = jnp.full((N_HEADS * tq, 1), slopes_ref[0], F32)
    for h in range(1, N_HEADS):
        s = jnp.where(hrow == h, slopes_ref[h], s)
    return s


def _softmax_rows(s, mask):
    s = jnp.where(mask, s, NEG)
    m = jnp.max(s, axis=1, keepdims=True)
    p = jnp.where(mask, jnp.exp(s - m), 0.0)
    return p / jnp.maximum(jnp.sum(p, axis=1, keepdims=True), 1e-30)


def _nsa_body(slopes_ref, q_ref, rows_ref, win_ref, small_ref, pw_ref, pair_ref, exp_ref, g_ref, o_ref,
              kvc_ref, m_ref, l_ref, acc_ref, *, tq, tk, n):
    qi = pl.program_id(1)
    hd, nh = HEAD_DIM, N_HEADS
    scale = hd ** -0.5
    n_cmp, n_sel = n // CMP_BLOCK, n // SEL_BLOCK
    r4 = nh * tq

    @pl.when(qi == 0)
    def _():
        kv = rows_ref[:, 0:2 * hd].reshape(n_cmp, CMP_BLOCK, 2 * hd)
        kvc_ref[...] = jnp.sum(kv * pw_ref[...][None], axis=1)

    q = q_ref[...] * scale
    qs = jnp.concatenate([q[:, h * hd:(h + 1) * hd] for h in range(nh)], axis=0).astype(BF16)
    slope = _slope_col(slopes_ref, tq)
    q0 = qi * tq
    t_row = q0 + lax.broadcasted_iota(jnp.int32, (r4, 1), 0) % tq
    t_q = q0 + lax.broadcasted_iota(jnp.int32, (tq, 1), 0)

    kvc = kvc_ref[...]
    s_c = _dot_nt(qs, kvc[:, 0:hd].astype(BF16))
    cmp_end = (lax.broadcasted_iota(jnp.int32, (1, n_cmp), 1) + 1) * CMP_BLOCK - 1
    dist_c = t_row - cmp_end
    p_c = _softmax_rows(s_c - slope * dist_c.astype(F32), dist_c >= 0)
    o_c = jnp.dot(p_c.astype(BF16), kvc[:, hd:2 * hd].astype(BF16), preferred_element_type=F32)

    psum = p_c[0:tq]
    for h in range(1, nh):
        psum = psum + p_c[h * tq:(h + 1) * tq]
    imp = lax.dot_general(pair_ref[...], psum, (((1,), (1,)), ((), ())), precision=HI,
                          preferred_element_type=F32)
    blk = lax.broadcasted_iota(jnp.int32, (n_sel, tq), 0)
    cur = (q0 + lax.broadcasted_iota(jnp.int32, (1, tq), 1)) // SEL_BLOCK
    valid = blk <= cur
    forced = (blk == cur) | (blk == 0)
    score = jnp.where(valid, jnp.where(forced, FORCE_SCORE, imp), -jnp.inf)
    rank = jnp.zeros((n_sel, tq), jnp.int32)
    for j in range(n_sel):
        sj = score[j:j + 1, :]
        beats = (sj > score) | ((sj == score) & (blk > j))
        rank = rank + beats.astype(jnp.int32)
    sel = ((rank < SEL_TOPK) & valid).astype(BF16)

    far = 1e9

    wlen = WINDOW + tq
    w0 = pl.multiple_of(jnp.maximum(q0 - WINDOW, 0), tq)
    wkv = win_ref[pl.ds(w0, wlen), :]
    dist_w = t_q - (w0 + lax.broadcasted_iota(jnp.int32, (1, wlen), 1))
    dmask_w = jnp.where((dist_w >= 0) & (dist_w < WINDOW), dist_w.astype(F32), far)
    s_w = _dot_nt(qs, wkv[:, 0:hd].astype(BF16)) - slope * jnp.concatenate([dmask_w] * nh, axis=0)
    p_w = jnp.exp(s_w - jnp.max(s_w, axis=1, keepdims=True))
    o_w = (jnp.dot(p_w.astype(BF16), wkv[:, hd:2 * hd].astype(BF16), preferred_element_type=F32)
           / jnp.sum(p_w, axis=1, keepdims=True))

    m_ref[...] = jnp.full(m_ref.shape, NEG, F32)
    l_ref[...] = jnp.zeros(l_ref.shape, F32)
    acc_ref[...] = jnp.zeros(acc_ref.shape, F32)
    lane_k = lax.broadcasted_iota(jnp.int32, (1, tk), 1)

    def body(kc, carry):
        r0 = pl.multiple_of(kc * tk, tk)
        kvs = rows_ref[pl.ds(r0, tk), 2 * hd:4 * hd]
        ks, vs = kvs[:, 0:hd].astype(BF16), kvs[:, hd:2 * hd].astype(BF16)
        dist = t_q - (r0 + lane_k)
        keep = _dot_tn(sel, exp_ref[kc])
        dmask = jnp.where((dist >= 0) & (keep > 0.5), dist.astype(F32), far)
        s = _dot_nt(qs, ks) - slope * jnp.concatenate([dmask] * nh, axis=0)
        m_old = m_ref[...]
        m_new = jnp.maximum(m_old, jnp.max(s, axis=1, keepdims=True))
        a = jnp.exp(m_old - m_new)
        p = jnp.exp(s - m_new)
        l_ref[...] = a * l_ref[...] + jnp.sum(p, axis=1, keepdims=True)
        acc_ref[...] = a * acc_ref[...] + jnp.dot(p.astype(BF16), vs, preferred_element_type=F32)
        m_ref[...] = m_new
        return carry

    lax.fori_loop(0, (q0 + tq + tk - 1) // tk, body, 0)
    o_s = acc_ref[...] / l_ref[...]

    gt = jax.nn.sigmoid(small_ref[:, S_NGATE:S_NGATE + 3 * nh])
    outs = []
    for h in range(nh):
        sl = slice(h * tq, (h + 1) * tq)
        outs.append(gt[:, 3 * h:3 * h + 1] * o_c[sl] + gt[:, 3 * h + 1:3 * h + 2] * o_s[sl]
                    + gt[:, 3 * h + 2:3 * h + 3] * o_w[sl])
    o_ref[...] = _rms(jnp.concatenate(outs, axis=1), g_ref[...])


def _nsa_prompt(proj, slopes, pool_w, g, bsz, n, tq=128, tk=1024):
    hd, nh, gw = HEAD_DIM, N_HEADS, GROUP_WIDTH
    assert n % tk == 0 and n >= WINDOW + tq and n % SEL_BLOCK == 0
    nq = n // tq
    n_cmp, n_sel = n // CMP_BLOCK, n // SEL_BLOCK
    pw = jnp.concatenate([jnp.broadcast_to(pool_w[0][:, None], (CMP_BLOCK, hd)),
                          jnp.broadcast_to(pool_w[1][:, None], (CMP_BLOCK, hd))], axis=1)
    pair = (np.arange(n_sel)[:, None] == np.arange(n_cmp)[None, :] // 2).astype(np.float32)
    expand = (np.arange(n_sel)[:, None] == np.arange(n)[None, :] // SEL_BLOCK)
    expand = jnp.asarray(expand.reshape(n_sel, n // tk, tk).transpose(1, 0, 2), BF16)
    return pl.pallas_call(
        functools.partial(_nsa_body, tq=tq, tk=tk, n=n),
        grid=(bsz, nq),
        in_specs=[pl.BlockSpec(memory_space=pltpu.SMEM),
                  pl.BlockSpec((tq, gw), lambda b, i: (b * nq + i, C_NQ // gw)),
                  pl.BlockSpec((n, gw), lambda b, i: (b, C_NROWS // gw)),
                  pl.BlockSpec((n, 2 * hd), lambda b, i: (b, C_WIN // (2 * hd))),
                  pl.BlockSpec((tq, 128), lambda b, i: (b * nq + i, C_SMALL // 128)),
                  _resident((CMP_BLOCK, 2 * hd)), _resident((n_sel, n_cmp)), _resident((n // tk, n_sel, tk)),
                  _resident((1, gw))],
        out_specs=pl.BlockSpec((tq, gw), lambda b, i: (b * nq + i, 0)),
        out_shape=jax.ShapeDtypeStruct((bsz * n, gw), F32),
        scratch_shapes=[pltpu.VMEM((n_cmp, 2 * hd), F32), pltpu.VMEM((nh * tq, 1), F32),
                        pltpu.VMEM((nh * tq, 1), F32), pltpu.VMEM((nh * tq, hd), F32)],
        compiler_params=_cparams(2),
        name="nsa_prompt",
    )(slopes, proj, proj, proj, proj, pw, jnp.asarray(pair), expand, g)


def _s5_body(u_ref, bre_ref, bim_ref, are_ref, aim_ref, s0r_ref, s0i_ref, cre_ref, cim_ref, d_ref, wg_ref, g_ref,
             o_ref, fr_ref, fi_ref, xr_ref, xi_ref, sr_ref, si_ref, *, bsz, tc):
    @pl.when(pl.program_id(0) == 0)
    def _():
        sr_ref[...] = s0r_ref[...]
        si_ref[...] = s0i_ref[...]

    u = u_ref[...]
    xr_ref[...] = _dotw(u, bre_ref[...])
    xi_ref[...] = _dotw(u, bim_ref[...])
    ar = jnp.broadcast_to(are_ref[...], sr_ref.shape)
    ai = jnp.broadcast_to(aim_ref[...], sr_ref.shape)

    def step(t, carry):
        sr, si = carry
        r0 = pl.multiple_of(t * bsz, bsz)
        nr = ar * sr - ai * si + xr_ref[pl.ds(r0, bsz), :]
        ni = ar * si + ai * sr + xi_ref[pl.ds(r0, bsz), :]
        xr_ref[pl.ds(r0, bsz), :] = nr
        xi_ref[pl.ds(r0, bsz), :] = ni
        return nr, ni

    sr, si = lax.fori_loop(0, tc, step, (sr_ref[...], si_ref[...]))
    sr_ref[...] = sr
    si_ref[...] = si
    fr_ref[...] = sr
    fi_ref[...] = si
    y = _dotw(xr_ref[...], cre_ref[...]) - _dotw(xi_ref[...], cim_ref[...])
    y = jax.nn.gelu(y + d_ref[...] * u)
    out = y * jax.nn.sigmoid(_dotw(y, wg_ref[...]))
    o_ref[...] = _rms(out, g_ref[...])


def _s5(u_tm, prm, s0r, s0i, g, bsz, n, tc):
    gw = GROUP_WIDTH
    ns = S5_GROUPS * S5_STATE
    rows = tc * bsz
    st = jax.ShapeDtypeStruct((bsz, ns), F32)
    return pl.pallas_call(
        functools.partial(_s5_body, bsz=bsz, tc=tc),
        grid=(n // tc,),
        in_specs=[pl.BlockSpec((rows, gw), lambda i: (i, 0)),
                  _resident((gw, ns)), _resident((gw, ns)), _resident((1, ns)), _resident((1, ns)),
                  _resident((bsz, ns)), _resident((bsz, ns)), _resident((ns, gw)), _resident((ns, gw)),
                  _resident((1, gw)), _resident((gw, gw)), _resident((1, gw))],
        out_specs=[pl.BlockSpec((rows, gw), lambda i: (i, 0)), pl.BlockSpec((bsz, ns), lambda i: (0, 0)),
                   pl.BlockSpec((bsz, ns), lambda i: (0, 0))],
        out_shape=[jax.ShapeDtypeStruct((n * bsz, gw), F32), st, st],
        scratch_shapes=[pltpu.VMEM((rows, ns), F32), pltpu.VMEM((rows, ns), F32),
                        pltpu.VMEM((bsz, ns), F32), pltpu.VMEM((bsz, ns), F32)],
        compiler_params=_cparams(1),
        name="s5",
    )(u_tm, prm["bre"], prm["bim"], prm["are"], prm["aim"], s0r, s0i, prm["cre"], prm["cim"], prm["d"], prm["wg"], g)


def _s5_params(a_re, a_im, b_re, b_im, c_re, c_im, d, log_dt, w_glu, wdt):
    dt = jnp.exp(log_dt)[:, None]
    mag = jnp.exp(dt * a_re)
    abar_re, abar_im = mag * jnp.cos(dt * a_im), mag * jnp.sin(dt * a_im)
    den = a_re * a_re + a_im * a_im
    num_re, num_im = abar_re - 1.0, abar_im
    zoh_re = (num_re * a_re + num_im * a_im) / den
    zoh_im = (num_im * a_re - num_re * a_im) / den
    bbar_re = zoh_re[..., None] * b_re - zoh_im[..., None] * b_im
    bbar_im = zoh_re[..., None] * b_im + zoh_im[..., None] * b_re
    eye = jnp.eye(S5_GROUPS, dtype=F32)
    ns = S5_GROUPS * S5_STATE
    to_in = lambda w: (w[:, :, None, :] * eye[:, None, :, None]).transpose(0, 3, 2, 1).reshape(GROUP_WIDTH, ns).astype(wdt)
    to_out = lambda w: (w[:, :, None, :] * eye[:, None, :, None]).transpose(0, 3, 2, 1).reshape(ns, GROUP_WIDTH).astype(wdt)
    return {"bre": to_in(bbar_re), "bim": to_in(bbar_im), "are": abar_re.reshape(1, ns), "aim": abar_im.reshape(1, ns),
            "cre": to_out(c_re), "cim": to_out(c_im), "d": d.reshape(1, GROUP_WIDTH), "wg": w_glu.astype(wdt)}


def _mm_bf16(a, b):
    return jnp.dot(a.astype(BF16), b.astype(BF16), preferred_element_type=F32)


def _l2n(x):
    return x * lax.rsqrt(jnp.sum(x * x, axis=-1, keepdims=True) + RMS_EPS)


def _gdn_body(qkv_ref, z_ref, small_ref, at_ref, cb_ref, s0_ref, cw_ref, alr_ref, dtr_ref, alc_ref, dtc_ref, ng_ref,
              o_ref, sfin_ref, ext_ref, s_ref, *, tc):
    j = pl.program_id(1)
    hd, nh, gw, ch = HEAD_DIM, N_HEADS, GROUP_WIDTH, GDN_CHUNK
    pad = 8

    @pl.when(j == 0)
    def _():
        ext_ref[0:pad, :] = cb_ref[0]
        s_ref[...] = s0_ref[0]

    @pl.when(j > 0)
    def _():
        ext_ref[0:pad, :] = ext_ref[tc:tc + pad, :]

    ext_ref[pad:tc + pad, :] = qkv_ref[...]
    base = pad - (CONV_W - 1)
    conv = ext_ref[base:base + tc, :] * cw_ref[0:1, :]
    for t in range(1, CONV_W):
        conv = conv + ext_ref[base + t:base + t + tc, :] * cw_ref[t:t + 1, :]
    conv = conv * jax.nn.sigmoid(conv)

    small = small_ref[...]
    g_col = -jnp.exp(alr_ref[...]) * jax.nn.softplus(small[:, S_GA:S_GA + nh] + dtr_ref[...])
    beta = jax.nn.sigmoid(small[:, S_GB:S_GB + nh])
    g_row = -jnp.exp(alc_ref[...]) * jax.nn.softplus(at_ref[0][0:nh, :] + dtc_ref[...])
    ri = lax.broadcasted_iota(jnp.int32, (tc, tc), 0)
    ci = lax.broadcasted_iota(jnp.int32, (tc, tc), 1)
    same = (ri // ch) == (ci // ch)
    tril = same & (ci <= ri)
    strict = same & (ci < ri)
    tril_f = tril.astype(F32)
    cum_col = jnp.dot(tril_f, g_col, precision=HI, preferred_element_type=F32)
    cum_row = lax.dot_general(g_row, tril_f, (((1,), (1,)), ((), ())), precision=HI,
                              preferred_element_type=F32)
    z = z_ref[...]
    hs = range(nh)
    gc = [cum_col[:, h:h + 1] for h in hs]
    decay = [jnp.where(tril, jnp.exp(jnp.where(tril, gc[h] - cum_row[h:h + 1, :], 0.0)), 0.0) for h in hs]
    q = [_l2n(conv[:, h * hd:(h + 1) * hd]) * hd ** -0.5 for h in hs]
    k = [_l2n(conv[:, gw + h * hd:gw + (h + 1) * hd]) for h in hs]
    bc = [beta[:, h:h + 1] for h in hs]
    kb = [k[h] * bc[h] for h in hs]
    kbf = [k[h].astype(BF16) for h in hs]
    m = [jnp.where(strict, _dot_nt(kb[h].astype(BF16), kbf[h]) * decay[h], 0.0) for h in hs]
    nmat = [-m[h] for h in hs]
    qm = m
    for _ in range(int(math.log2(ch)) - 1):
        qm = [_mm_bf16(qm[h], qm[h]) for h in hs]
        nmat = [nmat[h] + qm[h] + _mm_bf16(nmat[h], qm[h]) for h in hs]
    eg = [jnp.exp(gc[h]) for h in hs]
    rhs = [jnp.concatenate([conv[:, 2 * gw + h * hd:2 * gw + (h + 1) * hd] * bc[h], kb[h] * eg[h]], axis=1)
           for h in hs]
    uw = [rhs[h] + _mm_bf16(nmat[h], rhs[h]) for h in hs]
    attn = [(_dot_nt(q[h].astype(BF16), kbf[h]) * decay[h]).astype(BF16) for h in hs]
    qg = [(q[h] * eg[h]).astype(BF16) for h in hs]
    s = [s_ref[h] for h in hs]
    o_chunks = [[] for _ in hs]
    for c in range(tc // ch):
        r = slice(c * ch, (c + 1) * ch)
        sb = [s[h].astype(BF16) for h in hs]
        v_new = [uw[h][r, 0:hd] - jnp.dot(uw[h][r, hd:2 * hd].astype(BF16), sb[h], preferred_element_type=F32)
                 for h in hs]
        vb = [v_new[h].astype(BF16) for h in hs]
        for h in hs:
            o_chunks[h].append(jnp.dot(qg[h][r], sb[h], preferred_element_type=F32)
                               + jnp.dot(attn[h][r, r], vb[h], preferred_element_type=F32))
        g_last = [gc[h][(c + 1) * ch - 1:(c + 1) * ch, :] for h in hs]
        kd = [(k[h][r] * jnp.exp(g_last[h] - gc[h][r])).astype(BF16) for h in hs]
        s = [s[h] * jnp.exp(g_last[h]) + _dot_tn(kd[h], vb[h]) for h in hs]
    outs = []
    for h in hs:
        s_ref[h] = s[h]
        o = jnp.concatenate(o_chunks[h], axis=0)
        o = o * lax.rsqrt(jnp.mean(o * o, axis=-1, keepdims=True) + RMS_EPS) * ng_ref[:, h * hd:(h + 1) * hd]
        zh = z[:, h * hd:(h + 1) * hd]
        outs.append(o * (zh * jax.nn.sigmoid(zh)))
    o_ref[...] = jnp.concatenate(outs, axis=1)
    sfin_ref[0] = jnp.stack(s, axis=0)


def _gdn_prompt(proj, a_t, conv_buf8, s0, conv_w, a_log, dt_bias, norm_g, bsz, n, tc=256):
    hd, nh, gw = HEAD_DIM, N_HEADS, GROUP_WIDTH
    nb = n // tc
    return pl.pallas_call(
        functools.partial(_gdn_body, tc=tc),
        grid=(bsz, nb),
        in_specs=[pl.BlockSpec((tc, 3 * gw), lambda b, j: (b * nb + j, C_GQKV // (3 * gw))),
                  pl.BlockSpec((tc, gw), lambda b, j: (b * nb + j, C_GZ // gw)),
                  pl.BlockSpec((tc, 128), lambda b, j: (b * nb + j, C_SMALL // 128)),
                  pl.BlockSpec((1, 8, tc), lambda b, j: (b, 0, j)),
                  pl.BlockSpec((1, 8, 3 * gw), lambda b, j: (b, 0, 0)),
                  pl.BlockSpec((1, nh, hd, hd), lambda b, j: (b, 0, 0, 0)),
                  _resident((CONV_W, 3 * gw)), _resident((1, nh)), _resident((1, nh)), _resident((nh, 1)),
                  _resident((nh, 1)), _resident((1, gw))],
        out_specs=[pl.BlockSpec((tc, gw), lambda b, j: (b * nb + j, 0)),
                   pl.BlockSpec((1, nh, hd, hd), lambda b, j: (b, 0, 0, 0))],
        out_shape=[jax.ShapeDtypeStruct((bsz * n, gw), F32), jax.ShapeDtypeStruct((bsz, nh, hd, hd), F32)],
        scratch_shapes=[pltpu.VMEM((tc + 8, 3 * gw), F32), pltpu.VMEM((nh, hd, hd), F32)],
        compiler_params=_cparams(2),
        name="gdn_prompt",
    )(proj, proj, proj, a_t, conv_buf8, s0, conv_w, a_log.reshape(1, nh), dt_bias.reshape(1, nh),
      a_log.reshape(nh, 1), dt_bias.reshape(nh, 1), norm_g)


def _head_ones():
    h = np.arange(GROUP_WIDTH) // HEAD_DIM
    return jnp.asarray((h[:, None] == h[None, :]).astype(np.float32))


def _head_scores(k4, q, ones):
    return jnp.dot(k4 * q, ones, precision=HI, preferred_element_type=F32) * HEAD_DIM ** -0.5


def _tile4(x):
    return jnp.concatenate([x] * N_HEADS, axis=1)


def _fox_dec_body(pt_ref, q_ref, kvn_ref, ffn_ref, bf_ref, g_ref, *rest, npg, nstep):
    kv_refs, lf_refs = rest[0:npg], rest[npg:2 * npg]
    o_ref, lfo_ref, m_ref, l_ref, acc_ref, car_ref = rest[2 * npg:]
    j = pl.program_id(1)
    gw, hd, nh = GROUP_WIDTH, HEAD_DIM, N_HEADS
    scale = hd ** -0.5
    q = q_ref[0]
    npos = kv_refs[0].shape[3]

    def per_head(x):
        return jnp.sum(x.reshape(nh, hd, x.shape[1]), axis=1)

    def spread(x):
        return jnp.broadcast_to(x[:, None, :], (nh, hd, x.shape[1])).reshape(gw, x.shape[1])

    @pl.when(j == 0)
    def _():
        lf_new = jax.nn.log_sigmoid(ffn_ref[0] + bf_ref[...])
        lfo_ref[0] = lf_new
        m_ref[...] = jnp.full(m_ref.shape, NEG, F32)
        l_ref[...] = jnp.zeros(l_ref.shape, F32)
        acc_ref[...] = jnp.zeros(acc_ref.shape, F32)
        car_ref[...] = jnp.broadcast_to(lf_new, car_ref.shape)

    ri = lax.broadcasted_iota(jnp.int32, (npos, npos), 0)
    ci = lax.broadcasted_iota(jnp.int32, (npos, npos), 1)
    later = (ri > ci).astype(F32)
    lfs = [lf_refs[i][0, 0] for i in range(npg)]
    qk = [per_head(kv_refs[i][0, 0, 0:gw, :] * q) * scale for i in range(npg)]
    suf = [jnp.dot(lfs[i], later, precision=HI, preferred_element_type=F32) for i in range(npg)]
    car = [car_ref[...]]
    for i in range(npg):
        car.append(car[i] + jnp.sum(lfs[i], axis=1, keepdims=True))
    car_ref[...] = car[npg]
    state = [(m_ref[c], l_ref[c], acc_ref[c]) for c in range(2)]
    for c in range(2):
        m, l, acc = state[c]
        for i in range(c, npg, 2):
            s = qk[i] + car[i] + suf[i]
            m_new = jnp.maximum(m, s)
            a = jnp.exp(m - m_new)
            p = jnp.exp(s - m_new)
            l = a * l + p
            acc = spread(a) * acc + spread(p) * kv_refs[i][0, 0, gw:2 * gw, :]
            m = m_new
        state[c] = (m, l, acc)
    for c in range(2):
        m_ref[c], l_ref[c], acc_ref[c] = state[c]

    @pl.when(j == nstep - 1)
    def _():
        kvn = kvn_ref[0]
        s_new = per_head(kvn[0:gw, :] * q) * scale
        m_tot = s_new
        for c in range(2):
            m_tot = jnp.maximum(m_tot, jnp.max(state[c][0], axis=1, keepdims=True))
        p_new = jnp.exp(s_new - m_tot)
        l_tot = p_new
        acc_tot = spread(p_new) * kvn[gw:2 * gw, :]
        for c in range(2):
            m, l, acc = state[c]
            w = jnp.exp(m - m_tot)
            l_tot = l_tot + jnp.sum(l * w, axis=1, keepdims=True)
            acc_tot = acc_tot + jnp.sum(acc * spread(w), axis=1, keepdims=True)
        o = acc_tot / spread(l_tot)
        o_ref[0] = o * lax.rsqrt(jnp.mean(o * o, axis=0, keepdims=True) + RMS_EPS) * g_ref[...]


def _fox_decode(q_col, kvn_col, ffn_col, page_table, kv_cache_t, lf_cache_t, layer, bf_col, g_col, npg=16):
    bsz = q_col.shape[0]
    n_pages = page_table.shape[1]
    gw, nh = GROUP_WIDTH, N_HEADS
    npos = kv_cache_t.shape[3]
    nstep = n_pages // npg
    page = lambda i: (lambda b, j, pt: (layer, pt[b, n_pages - 1 - (j * npg + i)], 0, 0))
    const2 = lambda b, j, pt: (0, 0)
    per_b = lambda b, j, pt: (b, 0, 0)
    grid_spec = pltpu.PrefetchScalarGridSpec(
        num_scalar_prefetch=1,
        grid=(bsz, nstep),
        in_specs=[pl.BlockSpec((1, gw, 1), per_b), pl.BlockSpec((1, 2 * gw, 1), per_b), pl.BlockSpec((1, nh, 1), per_b),
                  pl.BlockSpec((nh, 1), const2), pl.BlockSpec((gw, 1), const2)]
        + [pl.BlockSpec((1, 1, 2 * gw, npos), page(i)) for i in range(npg)]
        + [pl.BlockSpec((1, 1, nh, npos), page(i)) for i in range(npg)],
        out_specs=[pl.BlockSpec((1, gw, 1), per_b), pl.BlockSpec((1, nh, 1), per_b)],
        scratch_shapes=[pltpu.VMEM((2, nh, npos), F32), pltpu.VMEM((2, nh, npos), F32),
                        pltpu.VMEM((2, gw, npos), F32), pltpu.VMEM((nh, npos), F32)],
    )
    return pl.pallas_call(
        functools.partial(_fox_dec_body, npg=npg, nstep=nstep),
        grid_spec=grid_spec,
        out_shape=[jax.ShapeDtypeStruct((bsz, gw, 1), F32), jax.ShapeDtypeStruct((bsz, nh, 1), F32)],
        compiler_params=_cparams(2),
        name="fox_decode",
    )(page_table, q_col, kvn_col, ffn_col, bf_col, g_col, *([kv_cache_t] * npg), *([lf_cache_t] * npg))


def _nsa_dec_cmp_body(pt_ref, q_ref, pw_ref, slope_ref, ones_ref, *rest, npg, nstep, p0):
    pg_refs = rest[0:npg]
    oc_ref, idx_ref, kvc_ref, imp_ref = rest[npg:]
    j = pl.program_id(1)
    gw, hd, nh = GROUP_WIDTH, HEAD_DIM, N_HEADS
    per_page = 128 // CMP_BLOCK
    pw = pw_ref[...]
    pooled = [jnp.sum((pg_refs[i][0, 0].T * pw).reshape(per_page, CMP_BLOCK, 2 * hd), axis=1) for i in range(npg)]
    r0 = pl.multiple_of(j * (npg * per_page), npg * per_page)
    kvc_ref[pl.ds(r0, npg * per_page), :] = jnp.concatenate(pooled, axis=0)

    @pl.when(j == nstep - 1)
    def _():
        n_cmp = kvc_ref.shape[0]
        n_sel = n_cmp // 2
        q = q_ref[0]
        kvc = kvc_ref[...]
        cmp_end = (lax.broadcasted_iota(jnp.int32, (n_cmp, 1), 0) + 1) * CMP_BLOCK - 1
        dist = p0 - cmp_end
        s = _head_scores(_tile4(kvc[:, 0:hd]), q, ones_ref[...]) - slope_ref[...] * dist.astype(F32)
        ok = dist >= 0
        s = jnp.where(ok, s, NEG)
        p = jnp.where(ok, jnp.exp(s - jnp.max(s, axis=0, keepdims=True)), 0.0)
        p = p / jnp.maximum(jnp.sum(p, axis=0, keepdims=True), 1e-30)
        oc_ref[0] = jnp.sum(p * _tile4(kvc[:, hd:2 * hd]), axis=0, keepdims=True)
        imp = p[:, 0:1]
        for h in range(1, nh):
            imp = imp + p[:, h * hd:h * hd + 1]
        imp_ref[...] = jnp.broadcast_to(imp, imp_ref.shape)
        colm = imp_ref[pl.ds(0, n_sel, stride=2), :] + imp_ref[pl.ds(1, n_sel, stride=2), :]
        ri = lax.broadcasted_iota(jnp.int32, (n_sel, n_sel), 0)
        ci = lax.broadcasted_iota(jnp.int32, (n_sel, n_sel), 1)
        colm = jnp.where(ri == 0, FORCE_SCORE, colm)
        rowm = colm.T
        beats = (rowm > colm) | ((rowm == colm) & (ci < ri))
        rank = jnp.sum(beats.astype(F32), axis=1, keepdims=True)
        sel = (rank < SEL_TOPK - 1).astype(F32)
        pos = jnp.dot((ci < ri).astype(F32), jnp.broadcast_to(sel, (n_sel, n_sel)), preferred_element_type=F32)
        onehot = jnp.where((sel > 0.5) & (pos == ci.astype(F32)), ri.astype(F32), 0.0)
        idx_ref[0] = jnp.sum(onehot, axis=0, keepdims=True).astype(jnp.int32)


def _nsa_dec_cmp(proj3, page_table, cache2, layer, pw128, slope_e, p0, npg=16):
    bsz = proj3.shape[0]
    n_pages = page_table.shape[1]
    gw, hd = GROUP_WIDTH, HEAD_DIM
    nstep = n_pages // npg
    n_cmp = p0 // CMP_BLOCK
    assert p0 % SEL_BLOCK == 0 and n_cmp // 2 == 128 and p0 == n_pages * 128
    page = lambda i: (lambda b, j, pt: (layer, pt[b, j * npg + i], 0, 0))
    const2 = lambda b, j, pt: (0, 0)
    grid_spec = pltpu.PrefetchScalarGridSpec(
        num_scalar_prefetch=1,
        grid=(bsz, nstep),
        in_specs=[pl.BlockSpec((1, 1, gw), lambda b, j, pt: (b, 0, C_NQ // gw)),
                  pl.BlockSpec((128, 2 * hd), const2), pl.BlockSpec((1, gw), const2), pl.BlockSpec((gw, gw), const2)]
        + [pl.BlockSpec((1, 1, 2 * hd, cache2.shape[3]), page(i)) for i in range(npg)],
        out_specs=[pl.BlockSpec((1, 1, gw), lambda b, j, pt: (b, 0, 0)),
                   pl.BlockSpec((1, 1, 128), lambda b, j, pt: (b, 0, 0))],
        scratch_shapes=[pltpu.VMEM((n_cmp, 2 * hd), F32), pltpu.VMEM((n_cmp, 128), F32)],
    )
    return pl.pallas_call(
        functools.partial(_nsa_dec_cmp_body, npg=npg, nstep=nstep, p0=p0),
        grid_spec=grid_spec,
        out_shape=[jax.ShapeDtypeStruct((bsz, 1, gw), F32), jax.ShapeDtypeStruct((bsz, 1, 128), jnp.int32)],
        compiler_params=_cparams(2),
        name="nsa_dec_cmp",
    )(page_table, proj3, pw128, slope_e, _head_ones(), *([cache2] * npg))


def _nsa_dec_sel_body(pt_ref, idx_ref, q_ref, new_ref, wnew_ref, small_ref, oc_ref, win_ref, slope_ref, ones_ref,
                      eg_ref, g_ref, *rest, nsel, p0):
    blk_refs = rest[0:nsel]
    o_ref = rest[nsel]
    b = pl.program_id(0)
    hd = HEAD_DIM
    q = q_ref[0]
    ones = ones_ref[...]
    slope = slope_ref[...]
    new = new_ref[0]
    s_new = _head_scores(_tile4(new[:, 2 * hd:3 * hd]), q, ones)
    v_new = _tile4(new[:, 3 * hd:4 * hd])
    ss, vs = [], []
    r = lax.broadcasted_iota(jnp.int32, (SEL_BLOCK, 1), 0)
    for i in range(nsel):
        page_t = blk_refs[i][0, 0].T
        half = idx_ref[b, i] % (page_t.shape[0] // SEL_BLOCK)
        blk = page_t[0:SEL_BLOCK]
        for hh in range(1, page_t.shape[0] // SEL_BLOCK):
            blk = jnp.where(half == hh, page_t[hh * SEL_BLOCK:(hh + 1) * SEL_BLOCK], blk)
        dist = (p0 - idx_ref[b, i] * SEL_BLOCK) - r
        ss.append(_head_scores(_tile4(blk[:, 0:hd]), q, ones) - slope * dist.astype(F32))
        vs.append(_tile4(blk[:, hd:2 * hd]))
    s = jnp.concatenate(ss, axis=0)
    v = jnp.concatenate(vs, axis=0)
    m = jnp.maximum(jnp.max(s, axis=0, keepdims=True), s_new)
    p, p_new = jnp.exp(s - m), jnp.exp(s_new - m)
    o_s = (jnp.sum(p * v, axis=0, keepdims=True) + p_new * v_new) / (jnp.sum(p, axis=0, keepdims=True) + p_new)
    win = win_ref[0, 0].T
    wn = wnew_ref[0]
    nw = win.shape[0]
    dist_w = nw - lax.broadcasted_iota(jnp.int32, (nw, 1), 0)
    ok = dist_w < WINDOW
    s_w = jnp.where(ok, _head_scores(_tile4(win[:, 0:hd]), q, ones) - slope * dist_w.astype(F32), NEG)
    sw_new = _head_scores(_tile4(wn[:, 0:hd]), q, ones)
    mw = jnp.maximum(jnp.max(s_w, axis=0, keepdims=True), sw_new)
    pw_, pw_new = jnp.where(ok, jnp.exp(s_w - mw), 0.0), jnp.exp(sw_new - mw)
    o_w = ((jnp.sum(pw_ * _tile4(win[:, hd:2 * hd]), axis=0, keepdims=True) + pw_new * _tile4(wn[:, hd:2 * hd]))
           / (jnp.sum(pw_, axis=0, keepdims=True) + pw_new))
    gt = jax.nn.sigmoid(small_ref[0])
    ge = [jnp.dot(gt, eg_ref[c], precision=HI, preferred_element_type=F32) for c in range(3)]
    o_ref[0] = _rms(ge[0] * oc_ref[0] + ge[1] * o_s + ge[2] * o_w, g_ref[...])


def _nsa_dec_sel(proj3, page_table, idx, o_c, cache, win_state, layer, slope_e, g, p0):
    bsz = proj3.shape[0]
    gw, hd, nh = GROUP_WIDTH, HEAD_DIM, N_HEADS
    nsel = SEL_TOPK - 1
    nw = win_state.shape[3]
    eg = np.zeros((3, 128, gw), np.float32)
    for c in range(3):
        for h in range(nh):
            eg[c, S_NGATE + 3 * h + c, h * hd:(h + 1) * hd] = 1.0
    n_pool = cache.shape[1]
    per_page = cache.shape[3] // SEL_BLOCK

    def blk(i):
        def index_map(b, pt, ix):
            sel = jnp.clip(ix[b, i], 0, p0 // SEL_BLOCK - 1)
            page = jnp.clip(pt[b, sel // per_page], 0, n_pool - 1)
            return (layer, page, 1, 0)
        return index_map
    const2 = lambda b, pt, ix: (0, 0)
    grid_spec = pltpu.PrefetchScalarGridSpec(
        num_scalar_prefetch=2,
        grid=(bsz,),
        in_specs=[pl.BlockSpec((1, 1, gw), lambda b, pt, ix: (b, 0, C_NQ // gw)),
                  pl.BlockSpec((1, 1, gw), lambda b, pt, ix: (b, 0, C_NROWS // gw)),
                  pl.BlockSpec((1, 1, 2 * hd), lambda b, pt, ix: (b, 0, C_WIN // (2 * hd))),
                  pl.BlockSpec((1, 1, 128), lambda b, pt, ix: (b, 0, C_SMALL // 128)),
                  pl.BlockSpec((1, 1, gw), lambda b, pt, ix: (b, 0, 0)),
                  pl.BlockSpec((1, 1, 2 * hd, nw), lambda b, pt, ix: (layer, b, 0, 0)),
                  pl.BlockSpec((1, gw), const2), pl.BlockSpec((gw, gw), const2),
                  pl.BlockSpec((3, 128, gw), lambda b, pt, ix: (0, 0, 0)), pl.BlockSpec((1, gw), const2)]
        + [pl.BlockSpec((1, 1, 2 * hd, cache.shape[3]), blk(i)) for i in range(nsel)],
        out_specs=pl.BlockSpec((1, 1, gw), lambda b, pt, ix: (b, 0, 0)),
    )
    return pl.pallas_call(
        functools.partial(_nsa_dec_sel_body, nsel=nsel, p0=p0),
        grid_spec=grid_spec,
        out_shape=jax.ShapeDtypeStruct((bsz, 1, gw), F32),
        compiler_params=_cparams(1),
        name="nsa_dec_sel",
    )(page_table, idx, proj3, proj3, proj3, proj3, o_c, win_state, slope_e, _head_ones(), jnp.asarray(eg), g,
      *([cache] * nsel))


def _gdn_dec_body(qkv_ref, z_ref, small_ref, cb_ref, s0_ref, cw_ref, al_ref, dt_ref, ng_ref, o_ref, s_ref):
    hd, nh, gw = HEAD_DIM, N_HEADS, GROUP_WIDTH
    cb = cb_ref[0, 0]
    conv = cb[0:1] * cw_ref[0:1, :]
    for t in range(1, CONV_W - 1):
        conv = conv + cb[t:t + 1] * cw_ref[t:t + 1, :]
    conv = conv + qkv_ref[0] * cw_ref[CONV_W - 1:CONV_W, :]
    conv = conv * jax.nn.sigmoid(conv)
    small = small_ref[0]
    g = -jnp.exp(al_ref[...]) * jax.nn.softplus(small[:, S_GA:S_GA + nh] + dt_ref[...])
    beta = jax.nn.sigmoid(small[:, S_GB:S_GB + nh])
    z = z_ref[0]
    ri = lax.broadcasted_iota(jnp.int32, (hd, hd), 0)
    ci = lax.broadcasted_iota(jnp.int32, (hd, hd), 1)
    outs = []
    for h in range(nh):
        q = _l2n(conv[:, h * hd:(h + 1) * hd]) * hd ** -0.5
        k = _l2n(conv[:, gw + h * hd:gw + (h + 1) * hd])
        v = conv[:, 2 * gw + h * hd:2 * gw + (h + 1) * hd]
        eg = jnp.exp(g[:, h:h + 1])
        bc = beta[:, h:h + 1]
        s = s0_ref[0, 0, h]
        v_new = v * bc - jnp.dot(k * bc * eg, s, precision=HI, preferred_element_type=F32)
        o = (jnp.dot(q * eg, s, precision=HI, preferred_element_type=F32)
             + jnp.sum(q * k, axis=-1, keepdims=True) * v_new)
        k_col = jnp.sum(jnp.where(ri == ci, jnp.broadcast_to(k, (hd, hd)), 0.0), axis=1, keepdims=True)
        s_ref[0, h] = s * eg + k_col * v_new
        o = o * lax.rsqrt(jnp.mean(o * o, axis=-1, keepdims=True) + RMS_EPS) * ng_ref[:, h * hd:(h + 1) * hd]
        zh = z[:, h * hd:(h + 1) * hd]
        outs.append(o * (zh * jax.nn.sigmoid(zh)))
    o_ref[0] = jnp.concatenate(outs, axis=1)


def _gdn_decode(proj3, conv_buf, s0, layer, conv_w, a_log, dt_bias, norm_g):
    bsz = proj3.shape[0]
    hd, nh, gw = HEAD_DIM, N_HEADS, GROUP_WIDTH
    const2 = lambda b: (0, 0)
    return pl.pallas_call(
        _gdn_dec_body,
        grid=(bsz,),
        in_specs=[pl.BlockSpec((1, 1, 3 * gw), lambda b: (b, 0, C_GQKV // (3 * gw))),
                  pl.BlockSpec((1, 1, gw), lambda b: (b, 0, C_GZ // gw)),
                  pl.BlockSpec((1, 1, 128), lambda b: (b, 0, C_SMALL // 128)),
                  pl.BlockSpec((1, 1, CONV_W - 1, 3 * gw), lambda b: (layer, b, 0, 0)),
                  pl.BlockSpec((1, 1, nh, hd, hd), lambda b: (layer, b, 0, 0, 0)),
                  pl.BlockSpec((CONV_W, 3 * gw), const2), pl.BlockSpec((1, nh), const2), pl.BlockSpec((1, nh), const2),
                  pl.BlockSpec((1, gw), const2)],
        out_specs=[pl.BlockSpec((1, 1, gw), lambda b: (b, 0, 0)), pl.BlockSpec((1, nh, hd, hd), lambda b: (b, 0, 0, 0))],
        out_shape=[jax.ShapeDtypeStruct((bsz, 1, gw), F32), jax.ShapeDtypeStruct((bsz, nh, hd, hd), F32)],
        compiler_params=_cparams(1),
        name="gdn_decode",
    )(proj3, proj3, proj3, conv_buf, s0, conv_w, a_log.reshape(1, nh), dt_bias.reshape(1, nh), norm_g)


def kernel(x_prompt, x_sample, cache_nsa_kv, cache_fox_kv, cache_fox_logf, state_nsa_win, state_gdn, state_gdn_conv, state_s5_re, state_s5_im, page_table, w_in, nsa_pool, s5_a_re, s5_a_im, s5_b_re, s5_b_im, s5_c_re, s5_c_im, s5_d, s5_log_dt, s5_w_glu, gdn_conv, gdn_a_log, gdn_dt_bias, fox_b_f, mix_norm, w_out, ln1_g, ln1_b, ln2_g, ln2_b, router_w, router_b, exp_w1, exp_w3, exp_w2):
    B, L, D = x_prompt.shape
    BS = x_sample.shape[0]
    hd, nh, gw = HEAD_DIM, N_HEADS, GROUP_WIDTH
    ns = S5_GROUPS * S5_STATE
    n_pool = cache_nsa_kv.shape[1]
    p0 = page_table.shape[1] * cache_nsa_kv.shape[2]
    slopes = 2.0 ** (-8.0 * (jnp.arange(nh, dtype=F32) + 1.0) / nh)
    slope_e = jnp.repeat(slopes, hd).reshape(1, gw)
    xp = x_prompt.reshape(B * L, D)
    xs = x_sample.reshape(BS, D)
    acc_p = [[] for _ in range(8)]
    acc_s = [[] for _ in range(8)]
    nsa_cache = cache_nsa_kv.transpose(0, 1, 3, 4, 2).reshape(DEPTH, n_pool, 4 * hd, -1)
    fox_cache_t = cache_fox_kv.transpose(0, 1, 3, 4, 5, 2).reshape(DEPTH, n_pool, 2 * gw, -1)
    fox_logf_t = cache_fox_logf.transpose(0, 1, 3, 2)
    win_all = state_nsa_win.transpose(0, 1, 3, 4, 2).reshape(DEPTH, BS, 2 * hd, -1)
    for l in range(DEPTH):
        wp_f = _permute_w_in(w_in[l])
        wo_f = w_out[l]
        wp, wo = wp_f.astype(BF16), wo_f.astype(BF16)
        w1 = exp_w1[l].astype(BF16).transpose(1, 0, 2).reshape(D, -1)
        w3 = exp_w3[l].astype(BF16).transpose(1, 0, 2).reshape(D, -1)
        w2 = exp_w2[l].astype(BF16).reshape(-1, D)
        rb = router_b.reshape(1, -1)
        ln1 = (ln1_g[l].reshape(1, D), ln1_b[l].reshape(1, D))
        ln2 = (ln2_g[l].reshape(1, D), ln2_b[l].reshape(1, D))
        g_nsa, g_s5, g_gdn, g_fox = [g.reshape(1, gw) for g in jnp.split(mix_norm[l], N_MIXERS)]
        s5_args = (s5_a_re[l], s5_a_im[l], s5_b_re[l], s5_b_im[l], s5_c_re[l], s5_c_im[l], s5_d[l],
                   s5_log_dt[l], s5_w_glu[l])
        prm, prm_f = _s5_params(*s5_args, BF16), _s5_params(*s5_args, F32)

        proj = _in_proj(xp, wp, 512)
        o_nsa = _nsa_prompt(proj, slopes, nsa_pool[l], g_nsa, B, L)
        small = proj[:, C_SMALL:C_SMALL + 128].reshape(B, L, 128)
        logf_t, cum_t = _fox_prep(small[:, :, S_FF:S_FF + nh].transpose(0, 2, 1), fox_b_f[l].reshape(nh, 1))
        o_fox = _fox_prompt(proj, cum_t, g_fox, B, L, 256)
        u_tm = proj[:, C_S5U:C_S5U + gw].reshape(B, L, gw).transpose(1, 0, 2).reshape(L * B, gw)
        zst = jnp.zeros((B, ns), F32)
        o_s5, s5r, s5i = _s5(u_tm, prm, zst, zst, g_s5, B, L, 256)
        o_s5 = o_s5.reshape(L, B, gw).transpose(1, 0, 2).reshape(B * L, gw)
        a_t = jnp.pad(small[:, :, S_GA:S_GA + nh].transpose(0, 2, 1), ((0, 0), (0, 8 - nh), (0, 0)))
        o_gdn, gdn_st = _gdn_prompt(proj, a_t, jnp.zeros((B, 8, 3 * gw), F32), jnp.zeros((B, nh, hd, hd), F32),
                                    gdn_conv[l], gdn_a_log[l], gdn_dt_bias[l], g_gdn, B, L)
        xp = _out_proj_ln(xp, [o_nsa, o_s5, o_gdn, o_fox], wo, *ln1, 512)
        xp = _moe_ln(xp, router_w, rb, w1, w3, w2, *ln2, 512)
        st = (proj[:, C_NROWS:C_NROWS + gw].reshape(B, L, 4, hd),
              proj[:, C_FKV:C_FKV + 2 * gw].reshape(B, L, 2, nh, hd),
              logf_t.transpose(0, 2, 1),
              proj[:, C_WIN:C_WIN + 2 * hd].reshape(B, L, 2, hd)[:, L - min(WINDOW, L):],
              gdn_st,
              proj[:, C_GQKV:C_GQKV + 3 * gw].reshape(B, L, 3 * gw)[:, L - (CONV_W - 1):],
              s5r.reshape(B, S5_GROUPS, S5_STATE), s5i.reshape(B, S5_GROUPS, S5_STATE))
        for a, v in zip(acc_p, st):
            a.append(v)

        proj_s = _in_proj(xs, wp_f, BS)
        proj3 = proj_s.reshape(BS, 1, PROJ_COLS)
        pw = jnp.concatenate([jnp.broadcast_to(nsa_pool[l][0][:, None], (CMP_BLOCK, hd)),
                              jnp.broadcast_to(nsa_pool[l][1][:, None], (CMP_BLOCK, hd))], axis=1)
        o_c, idx = _nsa_dec_cmp(proj3, page_table, nsa_cache, l, jnp.tile(pw, (128 // CMP_BLOCK, 1)), slope_e, p0)
        win_state = state_nsa_win[l]
        o_nsa = _nsa_dec_sel(proj3, page_table, idx.reshape(BS, 128), o_c, nsa_cache, win_all, l, slope_e, g_nsa, p0)
        ff_s = proj_s[:, C_SMALL + S_FF:C_SMALL + S_FF + nh]
        o_fox, lfo = _fox_decode(proj_s[:, C_FQ:C_FQ + gw].reshape(BS, gw, 1),
                                 proj_s[:, C_FKV:C_FKV + 2 * gw].reshape(BS, 2 * gw, 1), ff_s.reshape(BS, nh, 1),
                                 page_table, fox_cache_t, fox_logf_t, l, fox_b_f[l].reshape(nh, 1),
                                 g_fox.reshape(gw, 1))
        o_s5, s5r, s5i = _s5(proj_s[:, C_S5U:C_S5U + gw], prm_f, state_s5_re[l].reshape(BS, ns),
                             state_s5_im[l].reshape(BS, ns), g_s5, BS, 1, 1)
        o_gdn, gdn_st = _gdn_decode(proj3, state_gdn_conv, state_gdn, l, gdn_conv[l], gdn_a_log[l],
                                    gdn_dt_bias[l], g_gdn)
        xs = _out_proj_ln(xs, [o_nsa.reshape(BS, gw), o_s5, o_gdn.reshape(BS, gw), o_fox.reshape(BS, gw)], wo_f, *ln1, BS)
        xs = _moe_dec(xs, router_w, rb, exp_w1, exp_w3, exp_w2, l, *ln2)
        st = (proj_s[:, C_NROWS:C_NROWS + gw].reshape(BS, 1, 4, hd),
              proj_s[:, C_FKV:C_FKV + 2 * gw].reshape(BS, 1, 2, nh, hd),
              lfo.reshape(BS, 1, nh),
              jnp.concatenate([win_state[:, 1:], proj_s[:, C_WIN:C_WIN + 2 * hd].reshape(BS, 1, 2, hd)], axis=1),
              gdn_st,
              jnp.concatenate([state_gdn_conv[l][:, 1:], proj_s[:, C_GQKV:C_GQKV + 3 * gw].reshape(BS, 1, 3 * gw)], axis=1),
              s5r.reshape(BS, S5_GROUPS, S5_STATE), s5i.reshape(BS, S5_GROUPS, S5_STATE))
        for a, v in zip(acc_s, st):
            a.append(v)
    nsa_rows_p, fox_kv_p, fox_logf_p, nsa_win_p, gdn_p, gdn_conv_p, s5_re_p, s5_im_p = [jnp.stack(a) for a in acc_p]
    nsa_rows_s, fox_kv_s, fox_logf_s, nsa_win_s, gdn_s, gdn_conv_s, s5_re_s, s5_im_s = [jnp.stack(a) for a in acc_s]
    return (xp.reshape(B, L, D), xs.reshape(BS, 1, D), nsa_rows_p, nsa_rows_s, fox_kv_p, fox_kv_s, fox_logf_p, fox_logf_s,
            nsa_win_p, nsa_win_s, gdn_p, gdn_s, gdn_conv_p, gdn_conv_s, s5_re_p, s5_re_s, s5_im_p, s5_im_s)
```
